```python
import math
import jax, jax.numpy as jnp
from jax import lax
import numpy as np

D_MODEL = 1024
BATCH = 16
SEQ = 2048
DEPTH = 1

PLE_DIM = 256
RWKV_HEAD_DIM = 64
RWKV_WIDTH = D_MODEL // 2
RWKV_HEADS = RWKV_WIDTH // RWKV_HEAD_DIM
DECAY_LORA = 64
AAA_LORA = 64
GATE_LORA = 128
POOL_WINDOWS = (2, 4, 8, 16)
POOL_GROUPS = len(POOL_WINDOWS)
POOL_WIDTH = D_MODEL // 2
POOL_GROUP_DIM = POOL_WIDTH // POOL_GROUPS
N_BRANCHES = 2
D_FF = 4 * D_MODEL
RMS_EPS = 1e-6
GN_EPS = 64e-5
L2_EPS = 1e-12

RWKV_COLS = 3 * RWKV_WIDTH + DECAY_LORA + AAA_LORA + GATE_LORA
GATE_COLS = N_BRANCHES * D_MODEL
D_IN = RWKV_COLS + POOL_WIDTH + GATE_COLS

kernel_name = "hybrid_rwkv7_multipool_gated"


def rmsnorm(x, g):
    xf = x.astype(jnp.float32)
    y = xf * lax.rsqrt(jnp.mean(xf * xf, axis=-1, keepdims=True) + RMS_EPS)
    return (y * g.astype(jnp.float32)).astype(x.dtype)


def token_shift(z, mu):
    z_prev = jnp.pad(z[:, :-1], ((0, 0), (1, 0), (0, 0)))
    return z + (z_prev - z) * mu


def rwkv7_step(S, inp):
    r, w, k, v, kk, a = inp
    sa = jnp.einsum('bhvk,bhk->bhv', S, -kk)
    S = (S * w[:, :, None, :]
         + sa[..., None] * (kk * a)[:, :, None, :]
         + v[..., None] * k[:, :, None, :])
    y = jnp.einsum('bhvk,bhk->bhv', S, r)
    return S, y


def rwkv7_mix(zs, w0, w_decay_up, a0, w_aaa_up, w_gate_up, k_k, k_a, r_k, ln_w, ln_b):
    B, T, _ = zs.shape
    H, N = RWKV_HEADS, RWKV_HEAD_DIM
    zf = zs.astype(jnp.float32)
    o = 0
    r = zf[..., o:o + RWKV_WIDTH]; o += RWKV_WIDTH
    k = zf[..., o:o + RWKV_WIDTH]; o += RWKV_WIDTH
    v = zf[..., o:o + RWKV_WIDTH]; o += RWKV_WIDTH
    xw = zf[..., o:o + DECAY_LORA]; o += DECAY_LORA
    xa = zf[..., o:o + AAA_LORA]; o += AAA_LORA
    xg = zf[..., o:o + GATE_LORA]
    f32 = lambda t: t.astype(jnp.float32)

    w_log = -jax.nn.softplus(-(f32(w0) + jnp.tanh(xw) @ f32(w_decay_up))) - 0.5
    decay = jnp.exp(-jnp.exp(w_log))
    a = jax.nn.sigmoid(f32(a0) + xa @ f32(w_aaa_up))
    g = jax.nn.sigmoid(xg) @ f32(w_gate_up)

    heads = lambda t: t.reshape(B, T, H, N)
    kk = heads(k * f32(k_k))
    kk = kk / jnp.maximum(jnp.linalg.norm(kk, axis=-1, keepdims=True), L2_EPS)
    k = k * (1.0 + (a - 1.0) * f32(k_a))
    r_h, w_h, k_h, v_h, a_h = heads(r), heads(decay), heads(k), heads(v), heads(a)

    tm = lambda t: jnp.moveaxis(t, 1, 0)
    S0 = jnp.zeros((B, H, N, N), jnp.float32)
    _, y = lax.scan(rwkv7_step, S0,
                    (tm(r_h), tm(w_h), tm(k_h), tm(v_h), tm(kk), tm(a_h)))
    y = jnp.moveaxis(y, 0, 1)

    mean = jnp.mean(y, axis=-1, keepdims=True)
    var = jnp.mean(jnp.square(y - mean), axis=-1, keepdims=True)
    y = (y - mean) * lax.rsqrt(var + GN_EPS)
    y = y * f32(ln_w).reshape(H, N) + f32(ln_b).reshape(H, N)
    bonus = jnp.sum(r_h * k_h * f32(r_k), axis=-1, keepdims=True) * v_h
    out = (y + bonus).reshape(B, T, RWKV_WIDTH) * g
    return out.astype(zs.dtype)


def multiscale_pool_mix(z, pool_w, pool_scale):
    B, T, _ = z.shape
    zg = z.astype(jnp.float32).reshape(B, T, POOL_GROUPS, POOL_GROUP_DIM)
    cs = jnp.cumsum(zg, axis=1)
    pos = jnp.arange(T)
    outs = []
    for gi, win in enumerate(POOL_WINDOWS):
        c = cs[:, :, gi]
        c_prev = jnp.pad(c, ((0, 0), (win, 0), (0, 0)))[:, :T]
        count = jnp.minimum(pos + 1, win).astype(jnp.float32)
        outs.append((c - c_prev) / count[None, :, None] - zg[:, :, gi])
    pooled = jnp.stack(outs, axis=2)
    mixed = jnp.einsum('btgi,gio->btgo', pooled, pool_w.astype(jnp.float32))
    mixed = mixed.reshape(B, T, POOL_WIDTH) * pool_scale.astype(jnp.float32)
    return mixed.astype(z.dtype)


def setup_inputs(seed: int = 0) -> dict:
    key = jax.random.key(seed)
    ks = iter(jax.random.split(key, 40))
    nrm = lambda shape, s: jax.random.normal(next(ks), shape, jnp.float32) * s
    L = DEPTH
    gain = lambda n: 1.0 + nrm((L, n), 0.02)
    return {
        "x": nrm((BATCH, SEQ, D_MODEL), 1.0),
        "p": nrm((DEPTH, BATCH, SEQ, PLE_DIM), 1.0),
        "g_mix": gain(D_MODEL),
        "w_in": nrm((L, D_MODEL, D_IN), D_MODEL ** -0.5),
        "mu_shift": jax.random.uniform(next(ks), (L, RWKV_COLS), jnp.float32),
        "w0": -0.5 + nrm((L, RWKV_WIDTH), 0.5),
        "w_decay_up": nrm((L, DECAY_LORA, RWKV_WIDTH), 0.5 * DECAY_LORA ** -0.5),
        "a0": nrm((L, RWKV_WIDTH), 0.1),
        "w_aaa_up": nrm((L, AAA_LORA, RWKV_WIDTH), 0.5 * AAA_LORA ** -0.5),
        "w_gate_up": nrm((L, GATE_LORA, RWKV_WIDTH), GATE_LORA ** -0.5),
        "k_k": 0.85 + nrm((L, RWKV_WIDTH), 0.05),
        "k_a": 1.0 + nrm((L, RWKV_WIDTH), 0.05),
        "r_k": nrm((L, RWKV_HEADS, RWKV_HEAD_DIM), 0.1),
        "ln_x_w": gain(RWKV_WIDTH),
        "ln_x_b": nrm((L, RWKV_WIDTH), 0.01),
        "pool_w": nrm((L, POOL_GROUPS, POOL_GROUP_DIM, POOL_GROUP_DIM), POOL_GROUP_DIM ** -0.5),
        "pool_scale": 0.5 + nrm((L, POOL_WIDTH), 0.05),
        "b_gates": nrm((L, GATE_COLS), 0.01),
        "w_out_a": nrm((L, RWKV_WIDTH, D_MODEL), RWKV_WIDTH ** -0.5),
        "w_out_b": nrm((L, POOL_WIDTH, D_MODEL), POOL_WIDTH ** -0.5),
        "w_o": nrm((L, D_MODEL, D_MODEL), D_MODEL ** -0.5),
        "g_mlp": gain(D_MODEL),
        "w_ff1": nrm((L, D_MODEL, D_FF), D_MODEL ** -0.5),
        "w_ff2": nrm((L, D_FF, D_MODEL), D_FF ** -0.5),
        "g_ple": gain(D_MODEL),
        "w_ple_gate": nrm((L, D_MODEL, D_MODEL), D_MODEL ** -0.5),
        "w_ple_proj": nrm((L, PLE_DIM, D_MODEL), PLE_DIM ** -0.5),
        "g_final": 1.0 + nrm((D_MODEL,), 0.02),
    }


def reference(x, p, g_mix, w_in, mu_shift, w0, w_decay_up, a0, w_aaa_up, w_gate_up,
              k_k, k_a, r_k, ln_x_w, ln_x_b, pool_w, pool_scale, b_gates, w_out_a,
              w_out_b, w_o, g_mlp, w_ff1, w_ff2, g_ple, w_ple_gate, w_ple_proj, g_final):
    B, T, D = x.shape
    for i in range(DEPTH):
        h = rmsnorm(x, g_mix[i])
        z = h @ w_in[i]
        z_rwkv = token_shift(z[..., :RWKV_COLS], mu_shift[i])
        z_pool = z[..., RWKV_COLS:RWKV_COLS + POOL_WIDTH]
        gates = jax.nn.sigmoid(z[..., RWKV_COLS + POOL_WIDTH:] + b_gates[i])
        gates = gates.reshape(B, T, N_BRANCHES, D)
        y_a = rwkv7_mix(z_rwkv, w0[i], w_decay_up[i], a0[i], w_aaa_up[i], w_gate_up[i],
                        k_k[i], k_a[i], r_k[i], ln_x_w[i], ln_x_b[i]) @ w_out_a[i]
        y_b = multiscale_pool_mix(z_pool, pool_w[i], pool_scale[i]) @ w_out_b[i]
        merged = gates[:, :, 0] * y_a + gates[:, :, 1] * y_b
        x = x + merged @ w_o[i]
        h = rmsnorm(x, g_mlp[i])
        x = x + jnp.square(jax.nn.relu(h @ w_ff1[i])) @ w_ff2[i]
        h = rmsnorm(x, g_ple[i])
        x = x + jax.nn.sigmoid(h @ w_ple_gate[i]) * (p[i] @ w_ple_proj[i])
    return rmsnorm(x, g_final)
```

```python
import functools
import math

import jax
import jax.numpy as jnp
from jax import lax
from jax.experimental import pallas as pl
from jax.experimental.pallas import tpu as pltpu

F32 = jnp.float32
BF16 = jnp.bfloat16

D_MODEL = 1024
PLE_DIM = 256
HEAD_DIM = 64
RWKV_WIDTH = 512
HEAD_SHIFT = int(math.log2(HEAD_DIM))
N_HEADS = RWKV_WIDTH // HEAD_DIM
DECAY_LORA = 64
AAA_LORA = 64
GATE_LORA = 128
POOL_WINDOWS = (2, 4, 8, 16)
POOL_WIDTH = 512
POOL_GROUP_DIM = POOL_WIDTH // len(POOL_WINDOWS)
MAX_WINDOW = max(POOL_WINDOWS)
D_FF = 4 * D_MODEL
RMS_EPS = 1e-6
GN_EPS = 64e-5
L2_EPS = 1e-12
RWKV_COLS = 3 * RWKV_WIDTH + DECAY_LORA + AAA_LORA + GATE_LORA
GATE_COLS = 2 * D_MODEL
D_IN = RWKV_COLS + POOL_WIDTH + GATE_COLS
LORA_OFF = 3 * RWKV_WIDTH
GATE_LORA_OFF = LORA_OFF + DECAY_LORA + AAA_LORA

LANES = 128
SUBLANES = 8
HEADS_PER_BLOCK = LANES // HEAD_DIM
N_PAIRS = RWKV_WIDTH // LANES
CHUNK = 64
N_DOUBLINGS = int(math.log2(CHUNK)) - 1
LOG_DECAY_SCALE = -math.exp(-0.5)

INPROJ_ROWS = 512
POST_ROWS = 256
VMEM_LIMIT_BYTES = 56 * 1024 * 1024


def _head_of(channel):
    return lax.shift_right_logical(channel, HEAD_SHIFT)


def _mm(a, b):
    return jnp.dot(a.astype(BF16), b.astype(BF16), preferred_element_type=F32)


def _mm_nt(a, b):
    return lax.dot_general(a.astype(BF16), b.astype(BF16), (((1,), (1,)), ((), ())),
                           preferred_element_type=F32)


def _mm_tn(a, b):
    return lax.dot_general(a.astype(BF16), b.astype(BF16), (((0,), (0,)), ((), ())),
                           preferred_element_type=F32)


def _split3(x):
    hi = x.astype(BF16)
    r1 = x - hi.astype(F32)
    mid = r1.astype(BF16)
    lo = (r1 - mid.astype(F32)).astype(BF16)
    return hi, mid, lo


def _mm_exact_rhs(x, m):
    hi, mid, lo = _split3(x)
    d = lambda p: jnp.dot(p, m, preferred_element_type=F32)
    return d(hi) + (d(mid) + d(lo))


def _mm_exact_lhs(m, x):
    hi, mid, lo = _split3(x)
    d = lambda p: jnp.dot(m, p, preferred_element_type=F32)
    return d(hi) + (d(mid) + d(lo))


def _rmsnorm(x, g):
    return x * lax.rsqrt(jnp.mean(x * x, axis=-1, keepdims=True) + RMS_EPS) * g


def _inproj_kernel(x_ref, g_ref, w_ref, b_ref, zr_ref, zp_ref, gate_ref):
    h = _rmsnorm(x_ref[...], g_ref[...])
    z = jnp.dot(h.astype(BF16), w_ref[...], preferred_element_type=F32)
    zr_ref[...] = z[:, :RWKV_COLS]
    zp_ref[...] = z[:, RWKV_COLS:RWKV_COLS + POOL_WIDTH]
    gate_ref[...] = jax.nn.sigmoid(z[:, RWKV_COLS + POOL_WIDTH:] + b_ref[...])


def _inproj(x2d, g_mix, w_in_bf16, b_gates):
    m = x2d.shape[0]
    const = lambda shape: pl.BlockSpec(shape, lambda i: (0, 0), pipeline_mode=pl.Buffered(1))
    rows = lambda width: pl.BlockSpec((INPROJ_ROWS, width), lambda i: (i, 0))
    return pl.pallas_call(
        _inproj_kernel,
        grid=(m // INPROJ_ROWS,),
        in_specs=[rows(D_MODEL), const((1, D_MODEL)), const((D_MODEL, D_IN)), const((1, GATE_COLS))],
        out_specs=[rows(RWKV_COLS), rows(POOL_WIDTH), rows(GATE_COLS)],
        out_shape=[jax.ShapeDtypeStruct((m, RWKV_COLS), F32),
                   jax.ShapeDtypeStruct((m, POOL_WIDTH), F32),
                   jax.ShapeDtypeStruct((m, GATE_COLS), F32)],
        compiler_params=pltpu.CompilerParams(dimension_semantics=("arbitrary",),
                                             vmem_limit_bytes=VMEM_LIMIT_BYTES),
        name="inproj",
    )(x2d, g_mix, w_in_bf16, b_gates)


def _rwkv_kernel(zr_ref, mu_ref, w0_ref, wd_ref, a0_ref, wa_ref, wg_ref, kk_ref, ka_ref, rk_ref,
                 lnw_ref, lnb_ref, hsum_ref, tri_ref, out_ref, zbuf, g_state):
    t = pl.program_id(1)

    @pl.when(t == 0)
    def _():
        zbuf[0:SUBLANES, :] = jnp.zeros((SUBLANES, RWKV_COLS), F32)
        g_state[...] = jnp.zeros(g_state.shape, F32)

    z = zr_ref[...]
    zbuf[SUBLANES:SUBLANES + CHUNK, :] = z
    z_prev = zbuf[SUBLANES - 1:SUBLANES - 1 + CHUNK, :]
    zbuf[0:SUBLANES, :] = z[CHUNK - SUBLANES:, :]
    zs = z + (z_prev - z) * mu_ref[...]

    r = zs[:, 0:RWKV_WIDTH]
    k = zs[:, RWKV_WIDTH:2 * RWKV_WIDTH]
    v = zs[:, 2 * RWKV_WIDTH:3 * RWKV_WIDTH]
    x_lora = zs[:, LORA_OFF:LORA_OFF + LANES]
    xg = zs[:, GATE_LORA_OFF:GATE_LORA_OFF + GATE_LORA]

    logw = LOG_DECAY_SCALE * jax.nn.sigmoid(w0_ref[...] + _mm(jnp.tanh(x_lora), wd_ref[...]))
    a = jax.nn.sigmoid(a0_ref[...] + _mm(x_lora, wa_ref[...]))
    gate = _mm(jax.nn.sigmoid(xg), wg_ref[...])

    hsum = hsum_ref[...]
    head_sum = lambda y: _mm_exact_rhs(y, hsum)
    kk = k * kk_ref[...]
    kk = kk / jnp.maximum(jnp.sqrt(head_sum(kk * kk)), L2_EPS)
    k = k * (1.0 + (a - 1.0) * ka_ref[...])
    beta = kk * a

    cum = _mm_exact_lhs(tri_ref[...], logw)
    cum_last = cum[CHUNK - 1:CHUNK, :]
    e_pos = jnp.exp(cum)
    e_neg = jnp.exp(-cum)
    e_tail = jnp.exp(cum_last - cum)
    r_t = r * e_pos
    a_t = -kk * jnp.exp(cum - logw)
    b_t = beta * e_neg
    k_t = k * e_neg
    b_tail = beta * e_tail
    k_tail = k * e_tail
    p_chunk = jnp.exp(cum_last)

    row = lax.broadcasted_iota(jnp.int32, (CHUNK, CHUNK), 0)
    col = lax.broadcasted_iota(jnp.int32, (CHUNK, CHUNK), 1)
    strict_lower = col < row
    lower = col <= row
    eye = (col == row).astype(F32)
    lane = lax.broadcasted_iota(jnp.int32, (1, LANES), 1)
    brow = lax.broadcasted_iota(jnp.int32, (LANES, LANES), 0)
    bcol = lax.broadcasted_iota(jnp.int32, (LANES, LANES), 1)
    same_head = _head_of(brow) == _head_of(bcol)
    diag = brow == bcol

    ys = []
    for p in range(N_PAIRS):
        sl = slice(p * LANES, (p + 1) * LANES)
        r_p, a_p, b_p, k_p, v_p = r_t[:, sl], a_t[:, sl], b_t[:, sl], k_t[:, sl], v[:, sl]
        g_p = g_state[p]
        x_acc = None
        heads = []
        for h in range(HEADS_PER_BLOCK):
            m = (_head_of(lane) == h).astype(F32)
            a_h = a_p * m
            r_h = r_p * m
            v_h = v_p * m
            a_ab = jnp.where(strict_lower, _mm_nt(a_h, b_p), 0.0)
            a_ak = jnp.where(strict_lower, _mm_nt(a_h, k_p), 0.0)
            a_rb = jnp.where(lower, _mm_nt(r_h, b_p), 0.0)
            a_rk = jnp.where(lower, _mm_nt(r_h, k_p), 0.0)
            t_inv = eye + a_ab
            pw = a_ab
            for _ in range(N_DOUBLINGS):
                pw = _mm(pw, pw)
                t_inv = t_inv + _mm(t_inv, pw)
            rhs = jnp.concatenate([_mm(a_ak, v_h), a_h], axis=1)
            w12 = _mm(t_inv, rhs)
            x_acc = w12 if x_acc is None else x_acc + w12
            heads.append((m, a_rb, a_rk, v_h))
        w1, w2 = x_acc[:, :LANES], x_acc[:, LANES:]
        y1 = None
        r2 = r_p
        for m, a_rb, a_rk, v_h in heads:
            yr = _mm(a_rb, jnp.concatenate([w1 * m, w2 * m], axis=1))
            y_h = _mm(a_rk, v_h) + yr[:, :LANES]
            y1 = y_h if y1 is None else y1 + y_h
            r2 = r2 + yr[:, LANES:]
        ys.append(y1 + _mm(r2, g_p))
        bt_p = b_tail[:, sl]
        m_t = jnp.where(same_head, _mm_tn(bt_p, w2), 0.0) + jnp.where(diag, p_chunk[:, sl], 0.0)
        n_t = jnp.where(same_head, _mm_tn(bt_p, w1) + _mm_tn(k_tail[:, sl], v_p), 0.0)
        g_state[p] = _mm(m_t, g_p) + n_t

    y = jnp.concatenate(ys, axis=1)
    mean = head_sum(y) * (1.0 / HEAD_DIM)
    d = y - mean
    var = head_sum(d * d) * (1.0 / HEAD_DIM)
    yn = d * lax.rsqrt(var + GN_EPS) * lnw_ref[...] + lnb_ref[...]
    bonus = head_sum(r * k * rk_ref[...]) * v
    out_ref[...] = (yn + bonus) * gate


def _rwkv(zr, batch, seq, mu, w0, wd_pad, a0, wa_pad, wg, k_k, k_a, r_k, ln_w, ln_b, hsum, tri):
    n_chunks = seq // CHUNK
    const = lambda arr: pl.BlockSpec(arr.shape, lambda b, t: (0,) * arr.ndim,
                                     pipeline_mode=pl.Buffered(1))
    params = (mu, w0, wd_pad, a0, wa_pad, wg, k_k, k_a, r_k, ln_w, ln_b, hsum, tri)
    return pl.pallas_call(
        _rwkv_kernel,
        grid=(batch, n_chunks),
        in_specs=[pl.BlockSpec((CHUNK, RWKV_COLS), lambda b, t: (b * n_chunks + t, 0))]
                 + [const(p) for p in params],
        out_specs=pl.BlockSpec((CHUNK, RWKV_WIDTH), lambda b, t: (b * n_chunks + t, 0)),
        out_shape=jax.ShapeDtypeStruct((batch * seq, RWKV_WIDTH), F32),
        scratch_shapes=[pltpu.VMEM((CHUNK + SUBLANES, RWKV_COLS), F32),
                        pltpu.VMEM((N_PAIRS, LANES, LANES), F32)],
        compiler_params=pltpu.CompilerParams(dimension_semantics=("arbitrary", "arbitrary"),
                                             vmem_limit_bytes=VMEM_LIMIT_BYTES),
        name="rwkv7_mix",
    )(zr, *params)


def _post_kernel(x_ref, ya_ref, zp_ref, gate_ref, p_ref, poolw_ref, pscale_ref, woa_ref, wob_ref,
                 wo_ref, gmlp_ref, w1_ref, w2_ref, gple_ref, wpg_ref, wpe_ref, gfin_ref,
                 out_ref, pbuf, *, rows):
    t = pl.program_id(1)

    @pl.when(t == 0)
    def _():
        pbuf[0:MAX_WINDOW, :] = jnp.zeros((MAX_WINDOW, POOL_WIDTH), F32)

    zp = zp_ref[...]
    pbuf[MAX_WINDOW:MAX_WINDOW + rows, :] = zp
    pos = t * rows + lax.broadcasted_iota(jnp.int32, (rows, 1), 0)
    mixed = []
    for gi, win in enumerate(POOL_WINDOWS):
        cols = slice(gi * POOL_GROUP_DIM, (gi + 1) * POOL_GROUP_DIM)
        s = zp[:, cols]
        for j in range(1, win):
            s = s + pbuf[MAX_WINDOW - j:MAX_WINDOW - j + rows, cols]
        count = jnp.minimum(pos + 1, win).astype(F32)
        pooled = s / count - zp[:, cols]
        mixed.append(_mm(pooled, poolw_ref[gi]))
    pbuf[0:MAX_WINDOW, :] = zp[rows - MAX_WINDOW:, :]
    y_b = jnp.concatenate(mixed, axis=1) * pscale_ref[...]

    y_a = _mm(ya_ref[...], woa_ref[...])
    y_b = _mm(y_b, wob_ref[...])
    gate = gate_ref[...]
    merged = gate[:, :D_MODEL] * y_a + gate[:, D_MODEL:] * y_b
    x = x_ref[...] + _mm(merged, wo_ref[...])

    f = jnp.maximum(_mm(_rmsnorm(x, gmlp_ref[...]), w1_ref[...]), 0.0)
    x = x + _mm(f * f, w2_ref[...])

    h = _rmsnorm(x, gple_ref[...])
    x = x + jax.nn.sigmoid(_mm(h, wpg_ref[...])) * _mm(p_ref[...], wpe_ref[...])
    out_ref[...] = _rmsnorm(x, gfin_ref[...])


def _post(x2d, ya, zp, gates, p2d, batch, seq, params):
    n_t = seq // POST_ROWS
    rows = lambda width: pl.BlockSpec((POST_ROWS, width), lambda b, t: (b * n_t + t, 0))
    const = lambda arr: pl.BlockSpec(arr.shape, lambda b, t: (0,) * arr.ndim,
                                     pipeline_mode=pl.Buffered(1))
    return pl.pallas_call(
        functools.partial(_post_kernel, rows=POST_ROWS),
        grid=(batch, n_t),
        in_specs=[rows(D_MODEL), rows(RWKV_WIDTH), rows(POOL_WIDTH), rows(GATE_COLS), rows(PLE_DIM)]
                 + [const(p) for p in params],
        out_specs=rows(D_MODEL),
        out_shape=jax.ShapeDtypeStruct((batch * seq, D_MODEL), F32),
        scratch_shapes=[pltpu.VMEM((POST_ROWS + MAX_WINDOW, POOL_WIDTH), F32)],
        compiler_params=pltpu.CompilerParams(dimension_semantics=("arbitrary", "arbitrary"),
                                             vmem_limit_bytes=VMEM_LIMIT_BYTES),
        name="post_mix",
    )(x2d, ya, zp, gates, p2d, *params)


def _pad_rows(w, before, total):
    return jnp.pad(w, ((before, total - before - w.shape[0]), (0, 0)))


def kernel(x, p, g_mix, w_in, mu_shift, w0, w_decay_up, a0, w_aaa_up, w_gate_up, k_k, k_a, r_k, ln_x_w, ln_x_b, pool_w, pool_scale, b_gates, w_out_a, w_out_b, w_o, g_mlp, w_ff1, w_ff2, g_ple, w_ple_gate, w_ple_proj, g_final):
    batch, seq, d = x.shape
    depth = w_in.shape[0]
    assert d == D_MODEL and seq % POST_ROWS == 0 and (batch * seq) % INPROJ_ROWS == 0
    row = lambda a: a.reshape(1, -1)
    bf = lambda a: a.astype(BF16)

    head_id = jnp.arange(RWKV_WIDTH) // HEAD_DIM
    hsum = (head_id[:, None] == head_id[None, :]).astype(BF16)
    tri = (jnp.arange(CHUNK)[None, :] <= jnp.arange(CHUNK)[:, None]).astype(BF16)

    assert depth == 1, "the fused post kernel supports a single layer"
    i = 0
    x2d = x.reshape(batch * seq, d)
    zr, zp, gates = _inproj(x2d, row(g_mix[i]), bf(w_in[i]), row(b_gates[i]))
    ya = _rwkv(zr, batch, seq, row(mu_shift[i]), row(w0[i]),
               bf(_pad_rows(w_decay_up[i], 0, LANES)), row(a0[i]),
               bf(_pad_rows(w_aaa_up[i], DECAY_LORA, LANES)), bf(w_gate_up[i]),
               row(k_k[i]), row(k_a[i]), row(r_k[i]), row(ln_x_w[i]), row(ln_x_b[i]),
               hsum, tri)
    post_params = (bf(pool_w[i]), row(pool_scale[i]), bf(w_out_a[i]), bf(w_out_b[i]), bf(w_o[i]),
                   row(g_mlp[i]), bf(w_ff1[i]), bf(w_ff2[i]), row(g_ple[i]), bf(w_ple_gate[i]),
                   bf(w_ple_proj[i]), row(g_final))
    out = _post(x2d, ya, zp, gates, p[i].reshape(batch * seq, PLE_DIM), batch, seq, post_params)
    return out.reshape(batch, seq, d)
```

```python
import functools
import math

import jax
import jax.numpy as jnp
from jax import lax
from jax.experimental import pallas as pl
from jax.experimental.pallas import tpu as pltpu

F32 = jnp.float32
BF16 = jnp.bfloat16

D_MODEL = 1024
PLE_DIM = 256
HEAD_DIM = 64
RWKV_WIDTH = 512
HEAD_SHIFT = int(math.log2(HEAD_DIM))
N_HEADS = RWKV_WIDTH // HEAD_DIM
DECAY_LORA = 64
AAA_LORA = 64
GATE_LORA = 128
POOL_WINDOWS = (2, 4, 8, 16)
POOL_WIDTH = 512
POOL_GROUP_DIM = POOL_WIDTH // len(POOL_WINDOWS)
MAX_WINDOW = max(POOL_WINDOWS)
D_FF = 4 * D_MODEL
RMS_EPS = 1e-6
GN_EPS = 64e-5
L2_EPS = 1e-12
RWKV_COLS = 3 * RWKV_WIDTH + DECAY_LORA + AAA_LORA + GATE_LORA
GATE_COLS = 2 * D_MODEL
D_IN = RWKV_COLS + POOL_WIDTH + GATE_COLS
LORA_OFF = 3 * RWKV_WIDTH
GATE_LORA_OFF = LORA_OFF + DECAY_LORA + AAA_LORA

LANES = 128
SUBLANES = 8
HEADS_PER_BLOCK = LANES // HEAD_DIM
N_PAIRS = RWKV_WIDTH // LANES
HSUM_WIDTH = 256
CHUNK = 64
CHUNKS_PER_STEP = 4
RWKV_ROWS = CHUNK * CHUNKS_PER_STEP
N_DOUBLINGS = int(math.log2(CHUNK)) - 1
LOG_DECAY_SCALE = -math.exp(-0.5)

INPROJ_ROWS = 512
POST_ROWS = 256
VMEM_LIMIT_BYTES = 56 * 1024 * 1024


def _head_of(channel):
    return lax.shift_right_logical(channel, HEAD_SHIFT)


def _mm(a, b):
    return jnp.dot(a.astype(BF16), b.astype(BF16), preferred_element_type=F32)


def _mm_nt(a, b):
    return lax.dot_general(a.astype(BF16), b.astype(BF16), (((1,), (1,)), ((), ())),
                           preferred_element_type=F32)


def _mm_tn(a, b):
    return lax.dot_general(a.astype(BF16), b.astype(BF16), (((0,), (0,)), ((), ())),
                           preferred_element_type=F32)


def _split2(x):
    hi = x.astype(BF16)
    lo = (x - hi.astype(F32)).astype(BF16)
    return hi, lo


def _mm_exact_rhs(x, m):
    hi, lo = _split2(x)
    return jnp.dot(hi, m, preferred_element_type=F32) + jnp.dot(lo, m, preferred_element_type=F32)


def _mm_exact_lhs(m, x):
    hi, lo = _split2(x)
    return jnp.dot(m, hi, preferred_element_type=F32) + jnp.dot(m, lo, preferred_element_type=F32)


def _rmsnorm(x, g):
    return x * lax.rsqrt(jnp.mean(x * x, axis=-1, keepdims=True) + RMS_EPS) * g


def _inproj_kernel(x_ref, g_ref, w_ref, b_ref, zr_ref, zp_ref, gate_ref):
    h = _rmsnorm(x_ref[...], g_ref[...])
    z = jnp.dot(h.astype(BF16), w_ref[...], preferred_element_type=F32)
    zr_ref[...] = z[:, :RWKV_COLS]
    zp_ref[...] = z[:, RWKV_COLS:RWKV_COLS + POOL_WIDTH]
    gate_ref[...] = jax.nn.sigmoid(z[:, RWKV_COLS + POOL_WIDTH:] + b_ref[...])


def _inproj(x2d, g_mix, w_in_bf16, b_gates):
    m = x2d.shape[0]
    const = lambda shape: pl.BlockSpec(shape, lambda i: (0, 0), pipeline_mode=pl.Buffered(1))
    rows = lambda width: pl.BlockSpec((INPROJ_ROWS, width), lambda i: (i, 0))
    return pl.pallas_call(
        _inproj_kernel,
        grid=(m // INPROJ_ROWS,),
        in_specs=[rows(D_MODEL), const((1, D_MODEL)), const((D_MODEL, D_IN)), const((1, GATE_COLS))],
        out_specs=[rows(RWKV_COLS), rows(POOL_WIDTH), rows(GATE_COLS)],
        out_shape=[jax.ShapeDtypeStruct((m, RWKV_COLS), F32),
                   jax.ShapeDtypeStruct((m, POOL_WIDTH), F32),
                   jax.ShapeDtypeStruct((m, GATE_COLS), F32)],
        compiler_params=pltpu.CompilerParams(dimension_semantics=("arbitrary",),
                                             vmem_limit_bytes=VMEM_LIMIT_BYTES),
        name="inproj",
    )(x2d, g_mix, w_in_bf16, b_gates)


def _rwkv_kernel(zr_ref, mu_ref, w0_ref, wd_ref, a0_ref, wa_ref, wg_ref, kk_ref, ka_ref, rk_ref,
                 lnw_ref, lnb_ref, hsum_ref, tri_ref, out_ref, zbuf, g_state):
    t = pl.program_id(1)

    @pl.when(t == 0)
    def _():
        zbuf[0:SUBLANES, :] = jnp.zeros((SUBLANES, RWKV_COLS), F32)
        g_state[...] = jnp.zeros(g_state.shape, F32)

    z = zr_ref[...]
    zbuf[SUBLANES:SUBLANES + RWKV_ROWS, :] = z
    z_prev = zbuf[SUBLANES - 1:SUBLANES - 1 + RWKV_ROWS, :]
    zbuf[0:SUBLANES, :] = z[RWKV_ROWS - SUBLANES:, :]
    zs = z + (z_prev - z) * mu_ref[...]

    r = zs[:, 0:RWKV_WIDTH]
    k = zs[:, RWKV_WIDTH:2 * RWKV_WIDTH]
    v = zs[:, 2 * RWKV_WIDTH:3 * RWKV_WIDTH]
    x_lora = zs[:, LORA_OFF:LORA_OFF + LANES]
    xg = zs[:, GATE_LORA_OFF:GATE_LORA_OFF + GATE_LORA]

    logw = LOG_DECAY_SCALE * jax.nn.sigmoid(w0_ref[...] + _mm(jnp.tanh(x_lora), wd_ref[...]))
    a = jax.nn.sigmoid(a0_ref[...] + _mm(x_lora, wa_ref[...]))
    gate = _mm(jax.nn.sigmoid(xg), wg_ref[...])

    hsum = hsum_ref[...]
    head_sum = lambda y: jnp.concatenate(
        [_mm_exact_rhs(y[:, i:i + HSUM_WIDTH], hsum) for i in range(0, RWKV_WIDTH, HSUM_WIDTH)],
        axis=1)
    kk = k * kk_ref[...]
    kk = kk * jnp.minimum(lax.rsqrt(head_sum(kk * kk)), 1.0 / L2_EPS)
    k = k * (1.0 + (a - 1.0) * ka_ref[...])
    beta = kk * a

    cum = _mm_exact_lhs(tri_ref[...], logw)
    cum_last = jnp.concatenate(
        [jnp.broadcast_to(cum[(c + 1) * CHUNK - 1:(c + 1) * CHUNK, :], (CHUNK, RWKV_WIDTH))
         for c in range(CHUNKS_PER_STEP)], axis=0)
    e_neg = jnp.exp(-cum)
    e_tail = jnp.exp(cum_last - cum)
    r_t = r * jnp.exp(cum)
    a_t = -kk * jnp.exp(cum - logw)
    b_t = beta * e_neg
    k_t = k * e_neg
    b_tail = beta * e_tail
    k_tail = k * e_tail
    p_chunk = jnp.exp(cum_last)

    row = lax.broadcasted_iota(jnp.int32, (CHUNK, LANES), 0)
    col = lax.broadcasted_iota(jnp.int32, (CHUNK, LANES), 1) & (HEAD_DIM - 1)
    strict_lower = col < row
    lower = col <= row
    eye = (col == row).astype(F32)
    lane = lax.broadcasted_iota(jnp.int32, (1, LANES), 1)
    head_masks = [(_head_of(lane) == h).astype(BF16) for h in range(HEADS_PER_BLOCK)]
    head_masks2 = [jnp.concatenate([m, m], axis=1) for m in head_masks]
    brow = lax.broadcasted_iota(jnp.int32, (LANES, LANES), 0)
    bcol = lax.broadcasted_iota(jnp.int32, (LANES, LANES), 1)
    same_head = _head_of(brow) == _head_of(bcol)
    diag = brow == bcol

    def block_diag(x):
        xb = x.astype(BF16)
        masks = head_masks if x.shape[1] == LANES else head_masks2
        return jnp.concatenate([xb * m for m in masks], axis=0)

    blocks = [(slice(c * CHUNK, (c + 1) * CHUNK), slice(p * LANES, (p + 1) * LANES))
              for c in range(CHUNKS_PER_STEP) for p in range(N_PAIRS)]
    a_ps = [a_t[blk] for blk in blocks]
    r_ps = [r_t[blk] for blk in blocks]
    v_bd = [block_diag(v[blk]) for blk in blocks]
    tri4 = [_mm_nt(jnp.concatenate([x, y], axis=0),
                   jnp.concatenate([block_diag(b_t[blk]), block_diag(k_t[blk])], axis=0))
            for x, y, blk in zip(a_ps, r_ps, blocks)]
    a_ab = [jnp.where(strict_lower, x[:CHUNK, :LANES], 0.0) for x in tri4]
    a_ak = [jnp.where(strict_lower, x[:CHUNK, LANES:], 0.0) for x in tri4]
    a_rb = [jnp.where(lower, x[CHUNK:, :LANES], 0.0) for x in tri4]
    a_rk = [jnp.where(lower, x[CHUNK:, LANES:], 0.0) for x in tri4]
    x1 = [_mm(x, y) for x, y in zip(a_ak, v_bd)]
    t_inv = [eye + x for x in a_ab]
    pw = a_ab
    for _ in range(N_DOUBLINGS):
        pw = [_mm(x, block_diag(x)) for x in pw]
        t_inv = [x + _mm(x, block_diag(y)) for x, y in zip(t_inv, pw)]
    w12 = [_mm(x, block_diag(jnp.concatenate([y, z], axis=1)))
           for x, y, z in zip(t_inv, x1, a_ps)]
    w1 = [x[:, :LANES] for x in w12]
    w2 = [x[:, LANES:] for x in w12]
    yr = [_mm(x, block_diag(y)) for x, y in zip(a_rb, w12)]
    yk = [_mm(x, y) for x, y in zip(a_rk, v_bd)]
    y1 = [x + y[:, :LANES] for x, y in zip(yk, yr)]
    r2 = [x + y[:, LANES:] for x, y in zip(r_ps, yr)]
    m_t = [jnp.where(same_head, _mm_tn(b_tail[rs, cs], w2[q]), 0.0)
           + jnp.where(diag, p_chunk[rs, cs][0:1, :], 0.0)
           for q, (rs, cs) in enumerate(blocks)]
    n_t = [jnp.where(same_head, _mm_tn(b_tail[rs, cs], w1[q]) + _mm_tn(k_tail[rs, cs], v[rs, cs]), 0.0)
           for q, (rs, cs) in enumerate(blocks)]

    states = [g_state[p] for p in range(N_PAIRS)]
    y_rows = []
    for c in range(CHUNKS_PER_STEP):
        qs = range(c * N_PAIRS, (c + 1) * N_PAIRS)
        ys = [y1[q] + _mm(r2[q], states[p]) for p, q in enumerate(qs)]
        states = [_mm(m_t[q], states[p]) + n_t[q] for p, q in enumerate(qs)]
        y_rows.append(jnp.concatenate(ys, axis=1))
    for p in range(N_PAIRS):
        g_state[p] = states[p]

    y = jnp.concatenate(y_rows, axis=0)
    mean = head_sum(y) * (1.0 / HEAD_DIM)
    d = y - mean
    var = head_sum(d * d) * (1.0 / HEAD_DIM)
    yn = d * lax.rsqrt(var + GN_EPS) * lnw_ref[...] + lnb_ref[...]
    bonus = head_sum(r * k * rk_ref[...]) * v
    out_ref[...] = (yn + bonus) * gate


def _rwkv(zr, batch, seq, mu, w0, wd_pad, a0, wa_pad, wg, k_k, k_a, r_k, ln_w, ln_b, hsum, tri):
    n_t = seq // RWKV_ROWS
    const = lambda arr: pl.BlockSpec(arr.shape, lambda b, t: (0,) * arr.ndim,
                                     pipeline_mode=pl.Buffered(1))
    params = (mu, w0, wd_pad, a0, wa_pad, wg, k_k, k_a, r_k, ln_w, ln_b, hsum, tri)
    return pl.pallas_call(
        _rwkv_kernel,
        grid=(batch, n_t),
        in_specs=[pl.BlockSpec((RWKV_ROWS, RWKV_COLS), lambda b, t: (b * n_t + t, 0))]
                 + [const(p) for p in params],
        out_specs=pl.BlockSpec((RWKV_ROWS, RWKV_WIDTH), lambda b, t: (b * n_t + t, 0)),
        out_shape=jax.ShapeDtypeStruct((batch * seq, RWKV_WIDTH), F32),
        scratch_shapes=[pltpu.VMEM((RWKV_ROWS + SUBLANES, RWKV_COLS), F32),
                        pltpu.VMEM((N_PAIRS, LANES, LANES), F32)],
        compiler_params=pltpu.CompilerParams(dimension_semantics=("arbitrary", "arbitrary"),
                                             vmem_limit_bytes=VMEM_LIMIT_BYTES),
        name="rwkv7_mix",
    )(zr, *params)


def _post_kernel(x_ref, ya_ref, zp_ref, gate_ref, p_ref, poolw_ref, pscale_ref, woa_ref, wob_ref,
                 wo_ref, gmlp_ref, w1_ref, w2_ref, gple_ref, wpg_ref, wpe_ref, gfin_ref,
                 out_ref, pbuf, *, rows):
    t = pl.program_id(1)

    @pl.when(t == 0)
    def _():
        pbuf[0:MAX_WINDOW, :] = jnp.zeros((MAX_WINDOW, POOL_WIDTH), F32)

    zp = zp_ref[...]
    pbuf[MAX_WINDOW:MAX_WINDOW + rows, :] = zp
    pos = t * rows + lax.broadcasted_iota(jnp.int32, (rows, 1), 0)
    mixed = []
    for gi, win in enumerate(POOL_WINDOWS):
        cols = slice(gi * POOL_GROUP_DIM, (gi + 1) * POOL_GROUP_DIM)
        s = zp[:, cols]
        for j in range(1, win):
            s = s + pbuf[MAX_WINDOW - j:MAX_WINDOW - j + rows, cols]
        count = jnp.minimum(pos + 1, win).astype(F32)
        pooled = s / count - zp[:, cols]
        mixed.append(_mm(pooled, poolw_ref[gi]))
    pbuf[0:MAX_WINDOW, :] = zp[rows - MAX_WINDOW:, :]
    y_b = jnp.concatenate(mixed, axis=1) * pscale_ref[...]

    y_a = _mm(ya_ref[...], woa_ref[...])
    y_b = _mm(y_b, wob_ref[...])
    gate = gate_ref[...]
    merged = gate[:, :D_MODEL] * y_a + gate[:, D_MODEL:] * y_b
    x = x_ref[...] + _mm(merged, wo_ref[...])

    f = jnp.maximum(_mm(_rmsnorm(x, gmlp_ref[...]), w1_ref[...]), 0.0)
    x = x + _mm(f * f, w2_ref[...])

    h = _rmsnorm(x, gple_ref[...])
    x = x + jax.nn.sigmoid(_mm(h, wpg_ref[...])) * _mm(p_ref[...], wpe_ref[...])
    out_ref[...] = _rmsnorm(x, gfin_ref[...])


def _post(x2d, ya, zp, gates, p2d, batch, seq, params):
    n_t = seq // POST_ROWS
    rows = lambda width: pl.BlockSpec((POST_ROWS, width), lambda b, t: (b * n_t + t, 0))
    const = lambda arr: pl.BlockSpec(arr.shape, lambda b, t: (0,) * arr.ndim,
                                     pipeline_mode=pl.Buffered(1))
    return pl.pallas_call(
        functools.partial(_post_kernel, rows=POST_ROWS),
        grid=(batch, n_t),
        in_specs=[rows(D_MODEL), rows(RWKV_WIDTH), rows(POOL_WIDTH), rows(GATE_COLS), rows(PLE_DIM)]
                 + [const(p) for p in params],
        out_specs=rows(D_MODEL),
        out_shape=jax.ShapeDtypeStruct((batch * seq, D_MODEL), F32),
        scratch_shapes=[pltpu.VMEM((POST_ROWS + MAX_WINDOW, POOL_WIDTH), F32)],
        compiler_params=pltpu.CompilerParams(dimension_semantics=("arbitrary", "arbitrary"),
                                             vmem_limit_bytes=VMEM_LIMIT_BYTES),
        name="post_mix",
    )(x2d, ya, zp, gates, p2d, *params)


def _pad_rows(w, before, total):
    return jnp.pad(w, ((before, total - before - w.shape[0]), (0, 0)))


def kernel(x, p, g_mix, w_in, mu_shift, w0, w_decay_up, a0, w_aaa_up, w_gate_up, k_k, k_a, r_k, ln_x_w, ln_x_b, pool_w, pool_scale, b_gates, w_out_a, w_out_b, w_o, g_mlp, w_ff1, w_ff2, g_ple, w_ple_gate, w_ple_proj, g_final):
    batch, seq, d = x.shape
    depth = w_in.shape[0]
    assert d == D_MODEL and seq % POST_ROWS == 0 and seq % RWKV_ROWS == 0
    assert (batch * seq) % INPROJ_ROWS == 0
    row = lambda a: a.reshape(1, -1)
    bf = lambda a: a.astype(BF16)

    head_id = jnp.arange(HSUM_WIDTH) // HEAD_DIM
    hsum = (head_id[:, None] == head_id[None, :]).astype(BF16)
    tok = jnp.arange(RWKV_ROWS)
    same_chunk = tok[None, :] // CHUNK == tok[:, None] // CHUNK
    tri = ((tok[None, :] <= tok[:, None]) & same_chunk).astype(BF16)

    assert depth == 1, "the fused post kernel supports a single layer"
    i = 0
    x2d = x.reshape(batch * seq, d)
    zr, zp, gates = _inproj(x2d, row(g_mix[i]), bf(w_in[i]), row(b_gates[i]))
    ya = _rwkv(zr, batch, seq, row(mu_shift[i]), row(w0[i]),
               bf(_pad_rows(w_decay_up[i], 0, LANES)), row(a0[i]),
               bf(_pad_rows(w_aaa_up[i], DECAY_LORA, LANES)), bf(w_gate_up[i]),
               row(k_k[i]), row(k_a[i]), row(r_k[i]), row(ln_x_w[i]), row(ln_x_b[i]),
               hsum, tri)
    post_params = (bf(pool_w[i]), row(pool_scale[i]), bf(w_out_a[i]), bf(w_out_b[i]), bf(w_o[i]),
                   row(g_mlp[i]), bf(w_ff1[i]), bf(w_ff2[i]), row(g_ple[i]), bf(w_ple_gate[i]),
                   bf(w_ple_proj[i]), row(g_final))
    out = _post(x2d, ya, zp, gates, p[i].reshape(batch * seq, PLE_DIM), batch, seq, post_params)
    return out.reshape(batch, seq, d)
```

```python
import functools
import math

import jax
import jax.numpy as jnp
from jax import lax
from jax.experimental import pallas as pl
from jax.experimental.pallas import tpu as pltpu

F32 = jnp.float32
BF16 = jnp.bfloat16

D_MODEL = 1024
PLE_DIM = 256
HEAD_DIM = 64
RWKV_WIDTH = 512
HEAD_SHIFT = int(math.log2(HEAD_DIM))
DECAY_LORA = 64
AAA_LORA = 64
GATE_LORA = 128
POOL_WINDOWS = (2, 4, 8, 16)
POOL_WIDTH = 512
POOL_GROUP_DIM = POOL_WIDTH // len(POOL_WINDOWS)
MAX_WINDOW = max(POOL_WINDOWS)
D_FF = 4 * D_MODEL
RMS_EPS = 1e-6
GN_EPS = 64e-5
L2_EPS = 1e-12
RWKV_COLS = 3 * RWKV_WIDTH + DECAY_LORA + AAA_LORA + GATE_LORA
GATE_COLS = 2 * D_MODEL
D_IN = RWKV_COLS + POOL_WIDTH + GATE_COLS
LORA_OFF = 3 * RWKV_WIDTH
GATE_LORA_OFF = LORA_OFF + DECAY_LORA + AAA_LORA
GATE_OFF = RWKV_COLS + POOL_WIDTH

LANES = 128
SUBLANES = 8
HEADS_PER_BLOCK = LANES // HEAD_DIM
N_PAIRS = RWKV_WIDTH // LANES
HSUM_WIDTH = 256
CHUNK = 64
CHUNKS_PER_STEP = 4
RECUR_ROWS = CHUNK * CHUNKS_PER_STEP
N_DOUBLINGS = int(math.log2(CHUNK)) - 1
LOG_DECAY_SCALE = -math.exp(-0.5)

FRONT_ROWS = 512
PROJ_PIECE = 512
TAIL_ROWS = 512
VMEM_LIMIT_BYTES = 56 * 1024 * 1024


def _head_of(channel):
    return lax.shift_right_logical(channel, HEAD_SHIFT)


def _mm(a, b):
    return jnp.dot(a.astype(BF16), b.astype(BF16), preferred_element_type=F32)


def _mm_nt(a, b):
    return lax.dot_general(a.astype(BF16), b.astype(BF16), (((1,), (1,)), ((), ())),
                           preferred_element_type=F32)


def _mm_tn(a, b):
    return lax.dot_general(a.astype(BF16), b.astype(BF16), (((0,), (0,)), ((), ())),
                           preferred_element_type=F32)


def _chunk_cumsum(tri2, x):
    hi = x.astype(BF16)
    lo = (x - hi.astype(F32)).astype(BF16)
    return jnp.dot(tri2, jnp.concatenate([hi, lo], axis=0), preferred_element_type=F32)


def _head_sum(y, hsum):
    return jnp.concatenate(
        [_mm(y[:, i:i + HSUM_WIDTH], hsum) for i in range(0, RWKV_WIDTH, HSUM_WIDTH)], axis=1)


def _rmsnorm(x, g):
    return x * lax.rsqrt(jnp.mean(x * x, axis=-1, keepdims=True) + RMS_EPS) * g


def _front_kernel(x_ref, g_ref, w_ref, mu_ref, w0_ref, wd_ref, a0_ref, wa_ref, wg_ref,
                  kk_ref, ka_ref, rk_ref, hsum_ref, tri_ref,
                  rt_ref, at_ref, bt_ref, kt_ref, v_ref, btail_ref, ktail_ref, pc_ref,
                  bonus_ref, rgate_ref, zp_ref, zg_ref, zbuf, *, rows, tiles_per_seq):
    s = pl.program_id(0)
    t_lag = jnp.maximum(s - 1, 0) % tiles_per_seq

    @pl.when(s == 0)
    def _():
        zbuf[...] = jnp.zeros(zbuf.shape, F32)

    @pl.when(t_lag == 0)
    def _():
        zbuf[0:SUBLANES, :] = jnp.zeros((SUBLANES, RWKV_COLS), F32)

    z = zbuf[SUBLANES:SUBLANES + rows, :]
    z_prev = zbuf[SUBLANES - 1:SUBLANES - 1 + rows, :]
    zs = z + (z_prev - z) * mu_ref[...]
    r = zs[:, 0:RWKV_WIDTH]
    k = zs[:, RWKV_WIDTH:2 * RWKV_WIDTH]
    v = zs[:, 2 * RWKV_WIDTH:3 * RWKV_WIDTH]
    x_lora = zs[:, LORA_OFF:LORA_OFF + LANES]
    xg = zs[:, GATE_LORA_OFF:GATE_LORA_OFF + GATE_LORA]

    h = _rmsnorm(x_ref[...], g_ref[...]).astype(BF16)
    zbuf[0:SUBLANES, :] = z[rows - SUBLANES:, :]

    def project(first, last):
        y = jnp.dot(h, w_ref[:, first:last], preferred_element_type=F32)
        if last <= RWKV_COLS:
            zbuf[SUBLANES:SUBLANES + rows, first:last] = y
        elif last <= GATE_OFF:
            zp_ref[:, first - RWKV_COLS:last - RWKV_COLS] = y
        else:
            zg_ref[:, first - GATE_OFF:last - GATE_OFF] = y

    pieces = iter([(c, min(c + PROJ_PIECE, RWKV_COLS)) for c in range(0, RWKV_COLS, PROJ_PIECE)]
                  + [(c, c + PROJ_PIECE) for c in range(RWKV_COLS, D_IN, PROJ_PIECE)])

    def project_next(n):
        for _ in range(n):
            project(*next(pieces))

    project_next(1)
    logw = LOG_DECAY_SCALE * jax.nn.sigmoid(w0_ref[...] + _mm(jnp.tanh(x_lora), wd_ref[...]))
    a = jax.nn.sigmoid(a0_ref[...] + _mm(x_lora, wa_ref[...]))
    rgate_ref[...] = _mm(jax.nn.sigmoid(xg), wg_ref[...])
    project_next(1)

    hsum = hsum_ref[...]
    kk = k * kk_ref[...]
    kk = kk * jnp.minimum(lax.rsqrt(_head_sum(kk * kk, hsum)), 1.0 / L2_EPS)
    k = k * (1.0 + (a - 1.0) * ka_ref[...])
    beta = kk * a
    n_chunks = rows // CHUNK
    tri2 = tri_ref[...]
    cums = [_chunk_cumsum(tri2, logw[c * CHUNK:(c + 1) * CHUNK, :]) for c in range(n_chunks)]
    project_next(2)

    cum = jnp.concatenate(cums, axis=0)
    p_rows = [jnp.exp(x[CHUNK - 1:CHUNK, :]) for x in cums]
    p_full = jnp.concatenate([jnp.broadcast_to(x, (CHUNK, RWKV_WIDTH)) for x in p_rows], axis=0)
    pc_ref[...] = jnp.concatenate(
        [jnp.broadcast_to(x, (SUBLANES, RWKV_WIDTH)) for x in p_rows], axis=0)
    e_neg = jnp.exp(-cum)
    e_tail = p_full * e_neg
    rt_ref[...] = r * jnp.exp(cum)
    at_ref[...] = (-kk * jnp.exp(cum - logw)).astype(BF16)
    bt_ref[...] = (beta * e_neg).astype(BF16)
    kt_ref[...] = (k * e_neg).astype(BF16)
    btail_ref[...] = (beta * e_tail).astype(BF16)
    ktail_ref[...] = (k * e_tail).astype(BF16)
    v_ref[...] = v.astype(BF16)
    bonus_ref[...] = _head_sum(r * k * rk_ref[...], hsum) * v
    project_next(5)
    assert next(pieces, None) is None


def _front(x2d, batch, seq, params):
    assert (D_IN - RWKV_COLS) % PROJ_PIECE == 0
    tiles_per_seq = seq // FRONT_ROWS
    n_tiles = batch * tiles_per_seq
    m = batch * seq
    const = lambda arr: pl.BlockSpec(arr.shape, lambda s: (0,) * arr.ndim,
                                     pipeline_mode=pl.Buffered(1))
    cur = lambda s: (jnp.minimum(s, n_tiles - 1), 0)
    lag = lambda s: (jnp.maximum(s - 1, 0), 0)
    lagged = lambda width: pl.BlockSpec((FRONT_ROWS, width), lag)
    pc_rows = FRONT_ROWS // CHUNK * SUBLANES
    f32 = lambda r, w: jax.ShapeDtypeStruct((r, w), F32)
    bf16 = lambda r, w: jax.ShapeDtypeStruct((r, w), BF16)
    w = RWKV_WIDTH
    return pl.pallas_call(
        functools.partial(_front_kernel, rows=FRONT_ROWS, tiles_per_seq=tiles_per_seq),
        grid=(n_tiles + 1,),
        in_specs=[pl.BlockSpec((FRONT_ROWS, D_MODEL), cur)] + [const(p) for p in params],
        out_specs=[lagged(w)] * 7 + [pl.BlockSpec((pc_rows, w), lag)]
                  + [lagged(w), lagged(w),
                     pl.BlockSpec((FRONT_ROWS, POOL_WIDTH), cur),
                     pl.BlockSpec((FRONT_ROWS, GATE_COLS), cur)],
        out_shape=[f32(m, w)] + [bf16(m, w)] * 6 + [f32(m // CHUNK * SUBLANES, w)]
                  + [f32(m, w), f32(m, w), f32(m, POOL_WIDTH), f32(m, GATE_COLS)],
        scratch_shapes=[pltpu.VMEM((FRONT_ROWS + SUBLANES, RWKV_COLS), F32)],
        compiler_params=pltpu.CompilerParams(dimension_semantics=("arbitrary",),
                                             vmem_limit_bytes=VMEM_LIMIT_BYTES),
        name="front",
    )(x2d, *params)


def _recur_kernel(rt_ref, at_ref, bt_ref, kt_ref, v_ref, btail_ref, ktail_ref, pc_ref, bonus_ref,
                  rgate_ref, lnw_ref, lnb_ref, hsum_ref, out_ref, g_state):
    t = pl.program_id(1)

    @pl.when(t == 0)
    def _():
        g_state[...] = jnp.zeros(g_state.shape, F32)

    row = lax.broadcasted_iota(jnp.int32, (CHUNK, LANES), 0)
    col = lax.broadcasted_iota(jnp.int32, (CHUNK, LANES), 1) & (HEAD_DIM - 1)
    strict_lower = col < row
    lower = col <= row
    eye = (col == row).astype(F32)
    lane = lax.broadcasted_iota(jnp.int32, (1, LANES), 1)
    head_masks = [(_head_of(lane) == h).astype(BF16) for h in range(HEADS_PER_BLOCK)]
    head_masks2 = [jnp.concatenate([m, m], axis=1) for m in head_masks]
    brow = lax.broadcasted_iota(jnp.int32, (LANES, LANES), 0)
    bcol = lax.broadcasted_iota(jnp.int32, (LANES, LANES), 1)
    same_head = _head_of(brow) == _head_of(bcol)
    diag = brow == bcol

    def block_diag(x):
        xb = x.astype(BF16)
        masks = head_masks if x.shape[1] == LANES else head_masks2
        return jnp.concatenate([xb * m for m in masks], axis=0)

    blocks = [(slice(c * CHUNK, (c + 1) * CHUNK), slice(p * LANES, (p + 1) * LANES))
              for c in range(CHUNKS_PER_STEP) for p in range(N_PAIRS)]
    a_ps = [at_ref[blk] for blk in blocks]
    r_ps = [rt_ref[blk] for blk in blocks]
    v_ps = [v_ref[blk] for blk in blocks]
    v_bd = [block_diag(x) for x in v_ps]
    tri4 = [_mm_nt(jnp.concatenate([x, y.astype(BF16)], axis=0),
                   jnp.concatenate([block_diag(bt_ref[blk]), block_diag(kt_ref[blk])], axis=0))
            for x, y, blk in zip(a_ps, r_ps, blocks)]
    a_ab = [jnp.where(strict_lower, x[:CHUNK, :LANES], 0.0) for x in tri4]
    a_ak = [jnp.where(strict_lower, x[:CHUNK, LANES:], 0.0) for x in tri4]
    a_rb = [jnp.where(lower, x[CHUNK:, :LANES], 0.0) for x in tri4]
    a_rk = [jnp.where(lower, x[CHUNK:, LANES:], 0.0) for x in tri4]
    x1 = [_mm(x, y) for x, y in zip(a_ak, v_bd)]
    t_inv = [eye + x for x in a_ab]
    pw = a_ab
    for _ in range(N_DOUBLINGS):
        pw = [_mm(x, block_diag(x)) for x in pw]
        t_inv = [x + _mm(x, block_diag(y)) for x, y in zip(t_inv, pw)]
    w12 = [_mm(x, block_diag(jnp.concatenate([y.astype(BF16), z], axis=1)))
           for x, y, z in zip(t_inv, x1, a_ps)]
    w1 = [x[:, :LANES] for x in w12]
    w2 = [x[:, LANES:] for x in w12]
    p_last = [pc_ref[c * SUBLANES:c * SUBLANES + 1, cs]
              for c in range(CHUNKS_PER_STEP) for cs in (slice(p * LANES, (p + 1) * LANES)
                                                         for p in range(N_PAIRS))]
    m_t = [jnp.where(same_head, _mm_tn(btail_ref[blk], w2[q]), 0.0) + jnp.where(diag, p_last[q], 0.0)
           for q, blk in enumerate(blocks)]
    n_t = [jnp.where(same_head, _mm_tn(btail_ref[blk], w1[q]) + _mm_tn(ktail_ref[blk], v_ps[q]), 0.0)
           for q, blk in enumerate(blocks)]

    n_blocks = len(blocks)
    share = n_blocks // CHUNKS_PER_STEP * 2
    independent = ([(a_rb[q], block_diag(w12[q])) for q in range(n_blocks)]
                   + [(a_rk[q], v_bd[q]) for q in range(n_blocks)])
    products = []
    states = [[g_state[p] for p in range(N_PAIRS)]]
    for c in range(CHUNKS_PER_STEP):
        qs = range(c * N_PAIRS, (c + 1) * N_PAIRS)
        states.append([_mm(m_t[q], states[c][p]) + n_t[q] for p, q in enumerate(qs)])
        products += [_mm(x, y) for x, y in independent[c * share:(c + 1) * share]]
    for p in range(N_PAIRS):
        g_state[p] = states[CHUNKS_PER_STEP][p]
    yr, yk = products[:n_blocks], products[n_blocks:]
    ys = [yk[q] + yr[q][:, :LANES] + _mm(r_ps[q] + yr[q][:, LANES:], states[q // N_PAIRS][q % N_PAIRS])
          for q in range(n_blocks)]
    y = jnp.concatenate([jnp.concatenate(ys[c * N_PAIRS:(c + 1) * N_PAIRS], axis=1)
                         for c in range(CHUNKS_PER_STEP)], axis=0)

    hsum = hsum_ref[...]
    mean = _head_sum(y, hsum) * (1.0 / HEAD_DIM)
    d = y - mean
    var = _head_sum(d * d, hsum) * (1.0 / HEAD_DIM)
    yn = d * lax.rsqrt(var + GN_EPS) * lnw_ref[...] + lnb_ref[...]
    out_ref[...] = (yn + bonus_ref[...]) * rgate_ref[...]


def _recur(rt, at, bt, kt, v, btail, ktail, pc, bonus, rgate, batch, seq, ln_w, ln_b, hsum):
    n_t = seq // RECUR_ROWS
    const = lambda arr: pl.BlockSpec(arr.shape, lambda b, t: (0,) * arr.ndim,
                                     pipeline_mode=pl.Buffered(1))
    rows = pl.BlockSpec((RECUR_ROWS, RWKV_WIDTH), lambda b, t: (b * n_t + t, 0))
    pc_rows = pl.BlockSpec((CHUNKS_PER_STEP * SUBLANES, RWKV_WIDTH), lambda b, t: (b * n_t + t, 0))
    return pl.pallas_call(
        _recur_kernel,
        grid=(batch, n_t),
        in_specs=[rows] * 7 + [pc_rows, rows, rows, const(ln_w), const(ln_b), const(hsum)],
        out_specs=rows,
        out_shape=jax.ShapeDtypeStruct((batch * seq, RWKV_WIDTH), F32),
        scratch_shapes=[pltpu.VMEM((N_PAIRS, LANES, LANES), F32)],
        compiler_params=pltpu.CompilerParams(dimension_semantics=("arbitrary", "arbitrary"),
                                             vmem_limit_bytes=VMEM_LIMIT_BYTES),
        name="recur",
    )(rt, at, bt, kt, v, btail, ktail, pc, bonus, rgate, ln_w, ln_b, hsum)


def _tail_kernel(x_ref, ya_ref, zp_ref, zg_ref, p_ref, bg_ref, poolw_ref, pscale_ref, woa_ref, wob_ref,
                 wo_ref, gmlp_ref, w1_ref, w2_ref, gple_ref, wpg_ref, wpe_ref, gfin_ref,
                 out_ref, pbuf, *, rows):
    t = pl.program_id(1)

    @pl.when(t == 0)
    def _():
        pbuf[0:MAX_WINDOW, :] = jnp.zeros((MAX_WINDOW, POOL_WIDTH), F32)

    y_a = _mm(ya_ref[...], woa_ref[...])
    ple = _mm(p_ref[...], wpe_ref[...])

    zp = zp_ref[...]
    pbuf[MAX_WINDOW:MAX_WINDOW + rows, :] = zp
    pos = t * rows + lax.broadcasted_iota(jnp.int32, (rows, 1), 0)
    mixed = []
    for gi, win in enumerate(POOL_WINDOWS):
        cols = slice(gi * POOL_GROUP_DIM, (gi + 1) * POOL_GROUP_DIM)
        acc = zp[:, cols]
        for j in range(1, win):
            acc = acc + pbuf[MAX_WINDOW - j:MAX_WINDOW - j + rows, cols]
        count = jnp.minimum(pos + 1, win).astype(F32)
        mixed.append(_mm(acc / count - zp[:, cols], poolw_ref[gi]))
    pbuf[0:MAX_WINDOW, :] = zp[rows - MAX_WINDOW:, :]
    y_b = _mm(jnp.concatenate(mixed, axis=1) * pscale_ref[...], wob_ref[...])

    gate = jax.nn.sigmoid(zg_ref[...] + bg_ref[...])
    merged = gate[:, :D_MODEL] * y_a + gate[:, D_MODEL:] * y_b
    x = x_ref[...] + _mm(merged, wo_ref[...])

    f = jnp.maximum(_mm(_rmsnorm(x, gmlp_ref[...]), w1_ref[...]), 0.0)
    x = x + _mm(f * f, w2_ref[...])

    h = _rmsnorm(x, gple_ref[...])
    x = x + jax.nn.sigmoid(_mm(h, wpg_ref[...])) * ple
    out_ref[...] = _rmsnorm(x, gfin_ref[...])


def _tail(x2d, ya, zp, zg, p2d, batch, seq, params):
    n_t = seq // TAIL_ROWS
    rows = lambda width: pl.BlockSpec((TAIL_ROWS, width), lambda b, t: (b * n_t + t, 0))
    const = lambda arr: pl.BlockSpec(arr.shape, lambda b, t: (0,) * arr.ndim,
                                     pipeline_mode=pl.Buffered(1))
    return pl.pallas_call(
        functools.partial(_tail_kernel, rows=TAIL_ROWS),
        grid=(batch, n_t),
        in_specs=[rows(D_MODEL), rows(RWKV_WIDTH), rows(POOL_WIDTH), rows(GATE_COLS), rows(PLE_DIM)]
                 + [const(p) for p in params],
        out_specs=rows(D_MODEL),
        out_shape=jax.ShapeDtypeStruct((batch * seq, D_MODEL), F32),
        scratch_shapes=[pltpu.VMEM((TAIL_ROWS + MAX_WINDOW, POOL_WIDTH), F32)],
        compiler_params=pltpu.CompilerParams(dimension_semantics=("arbitrary", "arbitrary"),
                                             vmem_limit_bytes=VMEM_LIMIT_BYTES),
        name="tail",
    )(x2d, ya, zp, zg, p2d, *params)


def _pad_rows(w, before, total):
    return jnp.pad(w, ((before, total - before - w.shape[0]), (0, 0)))


def kernel(x, p, g_mix, w_in, mu_shift, w0, w_decay_up, a0, w_aaa_up, w_gate_up, k_k, k_a, r_k, ln_x_w, ln_x_b, pool_w, pool_scale, b_gates, w_out_a, w_out_b, w_o, g_mlp, w_ff1, w_ff2, g_ple, w_ple_gate, w_ple_proj, g_final):
    batch, seq, d = x.shape
    depth = w_in.shape[0]
    assert d == D_MODEL and seq % FRONT_ROWS == 0 and seq % RECUR_ROWS == 0
    assert seq % TAIL_ROWS == 0
    assert depth == 1, "the fused tail kernel supports a single layer"
    i = 0
    row = lambda a: a.reshape(1, -1)
    bf = lambda a: a.astype(BF16)

    head_id = jnp.arange(HSUM_WIDTH) // HEAD_DIM
    hsum = (head_id[:, None] == head_id[None, :]).astype(BF16)
    tok = jnp.arange(CHUNK)
    tri = (tok[None, :] <= tok[:, None]).astype(BF16)
    tri = jnp.concatenate([tri, tri], axis=1)

    x2d = x.reshape(batch * seq, d)
    front_params = (row(g_mix[i]), bf(w_in[i]), row(mu_shift[i]), row(w0[i]),
                    bf(_pad_rows(w_decay_up[i], 0, LANES)), row(a0[i]),
                    bf(_pad_rows(w_aaa_up[i], DECAY_LORA, LANES)), bf(w_gate_up[i]),
                    row(k_k[i]), row(k_a[i]), row(r_k[i]), hsum, tri)
    (rt, at, bt, kt, v, btail, ktail, pc, bonus, rgate, zp, zg) = _front(x2d, batch, seq,
                                                                         front_params)
    ya = _recur(rt, at, bt, kt, v, btail, ktail, pc, bonus, rgate, batch, seq,
                row(ln_x_w[i]), row(ln_x_b[i]), hsum)
    tail_params = (row(b_gates[i]), bf(pool_w[i]), row(pool_scale[i]), bf(w_out_a[i]),
                   bf(w_out_b[i]), bf(w_o[i]), row(g_mlp[i]), bf(w_ff1[i]), bf(w_ff2[i]),
                   row(g_ple[i]), bf(w_ple_gate[i]), bf(w_ple_proj[i]), row(g_final))
    out = _tail(x2d, ya, zp, zg, p[i].reshape(batch * seq, PLE_DIM), batch, seq, tail_params)
    return out.reshape(batch, seq, d)
```

```python
import functools
import math

import jax
import jax.numpy as jnp
from jax import lax
from jax.experimental import pallas as pl
from jax.experimental.pallas import tpu as pltpu

F32 = jnp.float32
BF16 = jnp.bfloat16

D_MODEL = 1024
PLE_DIM = 256
HEAD_DIM = 64
RWKV_WIDTH = 512
HEAD_SHIFT = int(math.log2(HEAD_DIM))
DECAY_LORA = 64
AAA_LORA = 64
GATE_LORA = 128
POOL_WINDOWS = (2, 4, 8, 16)
POOL_WIDTH = 512
POOL_GROUP_DIM = POOL_WIDTH // len(POOL_WINDOWS)
MAX_WINDOW = max(POOL_WINDOWS)
D_FF = 4 * D_MODEL
RMS_EPS = 1e-6
GN_EPS = 64e-5
L2_EPS = 1e-12
RWKV_COLS = 3 * RWKV_WIDTH + DECAY_LORA + AAA_LORA + GATE_LORA
GATE_COLS = 2 * D_MODEL
D_IN = RWKV_COLS + POOL_WIDTH + GATE_COLS
LORA_OFF = 3 * RWKV_WIDTH
GATE_LORA_OFF = LORA_OFF + DECAY_LORA + AAA_LORA
GATE_OFF = RWKV_COLS + POOL_WIDTH

LANES = 128
SUBLANES = 8
HEADS_PER_BLOCK = LANES // HEAD_DIM
N_PAIRS = RWKV_WIDTH // LANES
HSUM_WIDTH = 256
CHUNK = 64
CHUNKS_PER_STEP = 8
RECUR_ROWS = CHUNK * CHUNKS_PER_STEP
N_DOUBLINGS = int(math.log2(CHUNK)) - 1
LOG_DECAY_SCALE = -math.exp(-0.5)

FRONT_ROWS = 512
PROJ_PIECE = 512
TAIL_ROWS = 512
VMEM_LIMIT_BYTES = 56 * 1024 * 1024


def _head_of(channel):
    return lax.shift_right_logical(channel, HEAD_SHIFT)


def _mm(a, b):
    return jnp.dot(a.astype(BF16), b.astype(BF16), preferred_element_type=F32)


def _mm_nt(a, b):
    return lax.dot_general(a.astype(BF16), b.astype(BF16), (((1,), (1,)), ((), ())),
                           preferred_element_type=F32)


def _mm_tn(a, b):
    return lax.dot_general(a.astype(BF16), b.astype(BF16), (((0,), (0,)), ((), ())),
                           preferred_element_type=F32)


def _chunk_cumsum(tri2, x):
    hi = x.astype(BF16)
    lo = (x - hi.astype(F32)).astype(BF16)
    return jnp.dot(tri2, jnp.concatenate([hi, lo], axis=0), preferred_element_type=F32)


def _head_sum(y, hsum):
    return jnp.concatenate(
        [_mm(y[:, i:i + HSUM_WIDTH], hsum) for i in range(0, RWKV_WIDTH, HSUM_WIDTH)], axis=1)


def _rmsnorm(x, g):
    return x * lax.rsqrt(jnp.mean(x * x, axis=-1, keepdims=True) + RMS_EPS) * g


def _pool_group(gi, zp_ref, pbuf, t, rows):
    win = POOL_WINDOWS[gi]
    cols = slice(gi * POOL_GROUP_DIM, (gi + 1) * POOL_GROUP_DIM)
    tok = zp_ref[:, cols]
    acc = tok
    for j in range(1, win):
        acc = acc + pbuf[MAX_WINDOW - j:MAX_WINDOW - j + rows, cols]
    pos = t * rows + lax.broadcasted_iota(jnp.int32, (rows, 1), 0)
    count = jnp.minimum(pos + 1, win).astype(F32)
    return acc / count - tok


def _front_kernel(x_ref, g_ref, w_ref, mu_ref, w0_ref, wd_ref, a0_ref, wa_ref, wg_ref,
                  kk_ref, ka_ref, rk_ref, hsum_ref, tri_ref,
                  rt_ref, at_ref, bt_ref, kt_ref, v_ref, btail_ref, ktail_ref, pc_ref,
                  bonus_ref, rgate_ref, zp_ref, zg_ref, zbuf, *, rows, tiles_per_seq):
    s = pl.program_id(0)
    t_lag = jnp.maximum(s - 1, 0) % tiles_per_seq

    @pl.when(s == 0)
    def _():
        zbuf[...] = jnp.zeros(zbuf.shape, F32)

    @pl.when(t_lag == 0)
    def _():
        zbuf[0:SUBLANES, :] = jnp.zeros((SUBLANES, RWKV_COLS), F32)

    z = zbuf[SUBLANES:SUBLANES + rows, :]
    z_prev = zbuf[SUBLANES - 1:SUBLANES - 1 + rows, :]
    zs = z + (z_prev - z) * mu_ref[...]
    r = zs[:, 0:RWKV_WIDTH]
    k = zs[:, RWKV_WIDTH:2 * RWKV_WIDTH]
    v = zs[:, 2 * RWKV_WIDTH:3 * RWKV_WIDTH]
    x_lora = zs[:, LORA_OFF:LORA_OFF + LANES]
    xg = zs[:, GATE_LORA_OFF:GATE_LORA_OFF + GATE_LORA]

    h = _rmsnorm(x_ref[...], g_ref[...]).astype(BF16)
    zbuf[0:SUBLANES, :] = z[rows - SUBLANES:, :]

    def project(first, last):
        y = jnp.dot(h, w_ref[:, first:last], preferred_element_type=F32)
        if last <= RWKV_COLS:
            zbuf[SUBLANES:SUBLANES + rows, first:last] = y
        elif last <= GATE_OFF:
            zp_ref[:, first - RWKV_COLS:last - RWKV_COLS] = y
        else:
            zg_ref[:, first - GATE_OFF:last - GATE_OFF] = y

    pieces = iter([(c, min(c + PROJ_PIECE, RWKV_COLS)) for c in range(0, RWKV_COLS, PROJ_PIECE)]
                  + [(c, c + PROJ_PIECE) for c in range(RWKV_COLS, D_IN, PROJ_PIECE)])

    def project_next(n):
        for _ in range(n):
            project(*next(pieces))

    project_next(1)
    logw = LOG_DECAY_SCALE * jax.nn.sigmoid(w0_ref[...] + _mm(jnp.tanh(x_lora), wd_ref[...]))
    a = jax.nn.sigmoid(a0_ref[...] + _mm(x_lora, wa_ref[...]))
    rgate_ref[...] = _mm(jax.nn.sigmoid(xg), wg_ref[...])
    project_next(1)

    hsum = hsum_ref[...]
    kk = k * kk_ref[...]
    kk = kk * jnp.minimum(lax.rsqrt(_head_sum(kk * kk, hsum)), 1.0 / L2_EPS)
    k = k * (1.0 + (a - 1.0) * ka_ref[...])
    beta = kk * a
    n_chunks = rows // CHUNK
    tri2 = tri_ref[...]
    cums = [_chunk_cumsum(tri2, logw[c * CHUNK:(c + 1) * CHUNK, :]) for c in range(n_chunks)]
    project_next(2)

    cum = jnp.concatenate(cums, axis=0)
    p_rows = [jnp.exp(x[CHUNK - 1:CHUNK, :]) for x in cums]
    p_full = jnp.concatenate([jnp.broadcast_to(x, (CHUNK, RWKV_WIDTH)) for x in p_rows], axis=0)
    pc_ref[...] = jnp.concatenate(
        [jnp.broadcast_to(x, (SUBLANES, RWKV_WIDTH)) for x in p_rows], axis=0)
    e_neg = jnp.exp(-cum)
    e_tail = p_full * e_neg
    rt_ref[...] = r * jnp.exp(cum)
    at_ref[...] = (-kk * jnp.exp(cum - logw)).astype(BF16)
    bt_ref[...] = (beta * e_neg).astype(BF16)
    kt_ref[...] = (k * e_neg).astype(BF16)
    btail_ref[...] = (beta * e_tail).astype(BF16)
    ktail_ref[...] = (k * e_tail).astype(BF16)
    v_ref[...] = v.astype(BF16)
    bonus_ref[...] = _head_sum(r * k * rk_ref[...], hsum) * v
    project_next(5)
    assert next(pieces, None) is None


def _front(x2d, batch, seq, params):
    assert (D_IN - RWKV_COLS) % PROJ_PIECE == 0
    tiles_per_seq = seq // FRONT_ROWS
    n_tiles = batch * tiles_per_seq
    m = batch * seq
    const = lambda arr: pl.BlockSpec(arr.shape, lambda s: (0,) * arr.ndim,
                                     pipeline_mode=pl.Buffered(1))
    cur = lambda s: (jnp.minimum(s, n_tiles - 1), 0)
    lag = lambda s: (jnp.maximum(s - 1, 0), 0)
    lagged = lambda width: pl.BlockSpec((FRONT_ROWS, width), lag)
    pc_rows = FRONT_ROWS // CHUNK * SUBLANES
    f32 = lambda r, w: jax.ShapeDtypeStruct((r, w), F32)
    bf16 = lambda r, w: jax.ShapeDtypeStruct((r, w), BF16)
    w = RWKV_WIDTH
    return pl.pallas_call(
        functools.partial(_front_kernel, rows=FRONT_ROWS, tiles_per_seq=tiles_per_seq),
        grid=(n_tiles + 1,),
        in_specs=[pl.BlockSpec((FRONT_ROWS, D_MODEL), cur)] + [const(p) for p in params],
        out_specs=[lagged(w)] * 7 + [pl.BlockSpec((pc_rows, w), lag)]
                  + [lagged(w), lagged(w),
                     pl.BlockSpec((FRONT_ROWS, POOL_WIDTH), cur),
                     pl.BlockSpec((FRONT_ROWS, GATE_COLS), cur)],
        out_shape=[f32(m, w)] + [bf16(m, w)] * 6 + [f32(m // CHUNK * SUBLANES, w)]
                  + [f32(m, w), f32(m, w), f32(m, POOL_WIDTH), f32(m, GATE_COLS)],
        scratch_shapes=[pltpu.VMEM((FRONT_ROWS + SUBLANES, RWKV_COLS), F32)],
        compiler_params=pltpu.CompilerParams(dimension_semantics=("arbitrary",),
                                             vmem_limit_bytes=VMEM_LIMIT_BYTES),
        name="front",
    )(x2d, *params)


def _recur_kernel(rt_ref, at_ref, bt_ref, kt_ref, v_ref, btail_ref, ktail_ref, pc_ref, bonus_ref,
                  rgate_ref, zp_ref, lnw_ref, lnb_ref, hsum_ref, poolw_ref, pscale_ref,
                  out_ref, yb_ref, g_state, pbuf):
    t = pl.program_id(1)

    @pl.when(t == 0)
    def _():
        g_state[...] = jnp.zeros(g_state.shape, F32)
        pbuf[0:MAX_WINDOW, :] = jnp.zeros((MAX_WINDOW, POOL_WIDTH), F32)

    pbuf[MAX_WINDOW:MAX_WINDOW + RECUR_ROWS, :] = zp_ref[...]
    pooled = []


    row = lax.broadcasted_iota(jnp.int32, (CHUNK, LANES), 0)
    col = lax.broadcasted_iota(jnp.int32, (CHUNK, LANES), 1) & (HEAD_DIM - 1)
    strict_lower = col < row
    lower = col <= row
    eye = (col == row).astype(F32)
    lane = lax.broadcasted_iota(jnp.int32, (1, LANES), 1)
    head_masks = [(_head_of(lane) == h).astype(BF16) for h in range(HEADS_PER_BLOCK)]
    head_masks2 = [jnp.concatenate([m, m], axis=1) for m in head_masks]
    brow = lax.broadcasted_iota(jnp.int32, (LANES, LANES), 0)
    bcol = lax.broadcasted_iota(jnp.int32, (LANES, LANES), 1)
    same_head = _head_of(brow) == _head_of(bcol)
    diag = brow == bcol

    def block_diag(x):
        xb = x.astype(BF16)
        masks = head_masks if x.shape[1] == LANES else head_masks2
        return jnp.concatenate([xb * m for m in masks], axis=0)

    blocks = [(slice(c * CHUNK, (c + 1) * CHUNK), slice(p * LANES, (p + 1) * LANES))
              for c in range(CHUNKS_PER_STEP) for p in range(N_PAIRS)]
    a_ps = [at_ref[blk] for blk in blocks]
    r_ps = [rt_ref[blk] for blk in blocks]
    v_ps = [v_ref[blk] for blk in blocks]
    v_bd = [block_diag(x) for x in v_ps]
    tri4 = [_mm_nt(jnp.concatenate([x, y.astype(BF16)], axis=0),
                   jnp.concatenate([block_diag(bt_ref[blk]), block_diag(kt_ref[blk])], axis=0))
            for x, y, blk in zip(a_ps, r_ps, blocks)]
    a_ab = [jnp.where(strict_lower, x[:CHUNK, :LANES], 0.0) for x in tri4]
    a_ak = [jnp.where(strict_lower, x[:CHUNK, LANES:], 0.0) for x in tri4]
    a_rb = [jnp.where(lower, x[CHUNK:, :LANES], 0.0) for x in tri4]
    a_rk = [jnp.where(lower, x[CHUNK:, LANES:], 0.0) for x in tri4]
    stack = lambda x, y: jnp.concatenate([x.astype(BF16), y.astype(BF16)], axis=0)
    xk = [_mm(stack(x, y), z) for x, y, z in zip(a_ak, a_rk, v_bd)]
    x1 = [x[:CHUNK] for x in xk]
    yk = [x[CHUNK:] for x in xk]
    t_inv = [eye + x for x in a_ab]
    pw = [_mm(x, block_diag(x)) for x in a_ab]
    assert N_DOUBLINGS - 1 >= len(POOL_WINDOWS)
    for level in range(N_DOUBLINGS - 1):
        both = [_mm(stack(x, y), block_diag(x)) for x, y in zip(pw, t_inv)]
        pw = [x[:CHUNK] for x in both]
        t_inv = [x + y[CHUNK:] for x, y in zip(t_inv, both)]
        if level < len(POOL_WINDOWS):
            pooled.append(_mm(_pool_group(level, zp_ref, pbuf, t, RECUR_ROWS), poolw_ref[level]))
    pbuf[0:MAX_WINDOW, :] = zp_ref[RECUR_ROWS - MAX_WINDOW:, :]
    t_inv = [x + _mm(x, block_diag(y)) for x, y in zip(t_inv, pw)]
    w12 = [_mm(x, block_diag(jnp.concatenate([y.astype(BF16), z], axis=1)))
           for x, y, z in zip(t_inv, x1, a_ps)]
    p_last = [pc_ref[c * SUBLANES:c * SUBLANES + 1, cs]
              for c in range(CHUNKS_PER_STEP) for cs in (slice(p * LANES, (p + 1) * LANES)
                                                         for p in range(N_PAIRS))]
    no_w2 = jnp.zeros((CHUNK, LANES), BF16)
    mn = [_mm_tn(stack(btail_ref[blk], ktail_ref[blk]),
                 jnp.concatenate([jnp.concatenate([w12[q][:, LANES:], w12[q][:, :LANES]], axis=1)
                                  .astype(BF16), jnp.concatenate([no_w2, v_ps[q]], axis=1)], axis=0))
          for q, blk in enumerate(blocks)]
    m_t = [jnp.where(same_head, x[:, :LANES], 0.0) + jnp.where(diag, p_last[q], 0.0)
           for q, x in enumerate(mn)]
    n_t = [jnp.where(same_head, x[:, LANES:], 0.0) for x in mn]

    n_blocks = len(blocks)
    yr = []
    states = [[g_state[p] for p in range(N_PAIRS)]]
    for c in range(CHUNKS_PER_STEP):
        qs = range(c * N_PAIRS, (c + 1) * N_PAIRS)
        states.append([_mm(m_t[q], states[c][p]) + n_t[q] for p, q in enumerate(qs)])
        yr += [_mm(a_rb[q], block_diag(w12[q])) for q in qs]
    for p in range(N_PAIRS):
        g_state[p] = states[CHUNKS_PER_STEP][p]
    ys = [yk[q] + yr[q][:, :LANES] + _mm(r_ps[q] + yr[q][:, LANES:], states[q // N_PAIRS][q % N_PAIRS])
          for q in range(n_blocks)]
    y = jnp.concatenate([jnp.concatenate(ys[c * N_PAIRS:(c + 1) * N_PAIRS], axis=1)
                         for c in range(CHUNKS_PER_STEP)], axis=0)

    yb_ref[...] = jnp.concatenate(pooled, axis=1) * pscale_ref[...]

    hsum = hsum_ref[...]
    mean = _head_sum(y, hsum) * (1.0 / HEAD_DIM)
    d = y - mean
    var = _head_sum(d * d, hsum) * (1.0 / HEAD_DIM)
    yn = d * lax.rsqrt(var + GN_EPS) * lnw_ref[...] + lnb_ref[...]
    out_ref[...] = (yn + bonus_ref[...]) * rgate_ref[...]


def _recur(rt, at, bt, kt, v, btail, ktail, pc, bonus, rgate, zp, batch, seq, params):
    assert RWKV_WIDTH == POOL_WIDTH
    n_t = seq // RECUR_ROWS
    const = lambda arr: pl.BlockSpec(arr.shape, lambda b, t: (0,) * arr.ndim,
                                     pipeline_mode=pl.Buffered(1))
    rows = pl.BlockSpec((RECUR_ROWS, RWKV_WIDTH), lambda b, t: (b * n_t + t, 0))
    pc_rows = pl.BlockSpec((CHUNKS_PER_STEP * SUBLANES, RWKV_WIDTH), lambda b, t: (b * n_t + t, 0))
    out = jax.ShapeDtypeStruct((batch * seq, RWKV_WIDTH), F32)
    return pl.pallas_call(
        _recur_kernel,
        grid=(batch, n_t),
        in_specs=[rows] * 7 + [pc_rows, rows, rows, rows] + [const(p) for p in params],
        out_specs=[rows, rows],
        out_shape=[out, out],
        scratch_shapes=[pltpu.VMEM((N_PAIRS, LANES, LANES), F32),
                        pltpu.VMEM((RECUR_ROWS + MAX_WINDOW, POOL_WIDTH), F32)],
        compiler_params=pltpu.CompilerParams(dimension_semantics=("arbitrary", "arbitrary"),
                                             vmem_limit_bytes=VMEM_LIMIT_BYTES),
        name="recur",
    )(rt, at, bt, kt, v, btail, ktail, pc, bonus, rgate, zp, *params)


def _tail_kernel(x_ref, ya_ref, yb_ref, zg_ref, p_ref, bg_ref, woa_ref, wob_ref,
                 wo_ref, gmlp_ref, w1_ref, w2_ref, gple_ref, wpg_ref, wpe_ref, gfin_ref, out_ref):
    y_a = _mm(ya_ref[...], woa_ref[...])
    y_b = _mm(yb_ref[...], wob_ref[...])
    ple = _mm(p_ref[...], wpe_ref[...])
    gate = jax.nn.sigmoid(zg_ref[...] + bg_ref[...])
    merged = gate[:, :D_MODEL] * y_a + gate[:, D_MODEL:] * y_b
    x = x_ref[...] + _mm(merged, wo_ref[...])

    f = jnp.maximum(_mm(_rmsnorm(x, gmlp_ref[...]), w1_ref[...]), 0.0)
    x = x + _mm(f * f, w2_ref[...])

    h = _rmsnorm(x, gple_ref[...])
    x = x + jax.nn.sigmoid(_mm(h, wpg_ref[...])) * ple
    out_ref[...] = _rmsnorm(x, gfin_ref[...])


def _tail(x2d, ya, yb, zg, p2d, params):
    m = x2d.shape[0]
    rows = lambda width: pl.BlockSpec((TAIL_ROWS, width), lambda i: (i, 0))
    const = lambda arr: pl.BlockSpec(arr.shape, lambda i: (0,) * arr.ndim,
                                     pipeline_mode=pl.Buffered(1))
    return pl.pallas_call(
        _tail_kernel,
        grid=(m // TAIL_ROWS,),
        in_specs=[rows(D_MODEL), rows(RWKV_WIDTH), rows(POOL_WIDTH), rows(GATE_COLS), rows(PLE_DIM)]
                 + [const(p) for p in params],
        out_specs=rows(D_MODEL),
        out_shape=jax.ShapeDtypeStruct((m, D_MODEL), F32),
        compiler_params=pltpu.CompilerParams(dimension_semantics=("arbitrary",),
                                             vmem_limit_bytes=VMEM_LIMIT_BYTES),
        name="tail",
    )(x2d, ya, yb, zg, p2d, *params)


def _pad_rows(w, before, total):
    return jnp.pad(w, ((before, total - before - w.shape[0]), (0, 0)))


def kernel(x, p, g_mix, w_in, mu_shift, w0, w_decay_up, a0, w_aaa_up, w_gate_up, k_k, k_a, r_k, ln_x_w, ln_x_b, pool_w, pool_scale, b_gates, w_out_a, w_out_b, w_o, g_mlp, w_ff1, w_ff2, g_ple, w_ple_gate, w_ple_proj, g_final):
    batch, seq, d = x.shape
    depth = w_in.shape[0]
    assert d == D_MODEL and seq % FRONT_ROWS == 0 and seq % RECUR_ROWS == 0
    assert (batch * seq) % TAIL_ROWS == 0
    assert depth == 1, "the fused tail kernel supports a single layer"
    i = 0
    row = lambda a: a.reshape(1, -1)
    bf = lambda a: a.astype(BF16)

    head_id = jnp.arange(HSUM_WIDTH) // HEAD_DIM
    hsum = (head_id[:, None] == head_id[None, :]).astype(BF16)
    tok = jnp.arange(CHUNK)
    tri = (tok[None, :] <= tok[:, None]).astype(BF16)
    tri = jnp.concatenate([tri, tri], axis=1)

    x2d = x.reshape(batch * seq, d)
    front_params = (row(g_mix[i]), bf(w_in[i]), row(mu_shift[i]), row(w0[i]),
                    bf(_pad_rows(w_decay_up[i], 0, LANES)), row(a0[i]),
                    bf(_pad_rows(w_aaa_up[i], DECAY_LORA, LANES)), bf(w_gate_up[i]),
                    row(k_k[i]), row(k_a[i]), row(r_k[i]), hsum, tri)
    (rt, at, bt, kt, v, btail, ktail, pc, bonus, rgate, zp, zg) = _front(x2d, batch, seq,
                                                                         front_params)
    recur_params = (row(ln_x_w[i]), row(ln_x_b[i]), hsum, bf(pool_w[i]), row(pool_scale[i]))
    ya, yb = _recur(rt, at, bt, kt, v, btail, ktail, pc, bonus, rgate, zp, batch, seq, recur_params)
    tail_params = (row(b_gates[i]), bf(w_out_a[i]), bf(w_out_b[i]), bf(w_o[i]), row(g_mlp[i]),
                   bf(w_ff1[i]), bf(w_ff2[i]), row(g_ple[i]), bf(w_ple_gate[i]), bf(w_ple_proj[i]),
                   row(g_final))
    out = _tail(x2d, ya, yb, zg, p[i].reshape(batch * seq, PLE_DIM), tail_params)
    return out.reshape(batch, seq, d)
```

```python
import functools
import math

import jax
import jax.numpy as jnp
from jax import lax
from jax.experimental import pallas as pl
from jax.experimental.pallas import tpu as pltpu

F32 = jnp.float32
BF16 = jnp.bfloat16

D_MODEL = 1024
PLE_DIM = 256
HEAD_DIM = 64
RWKV_WIDTH = 512
HEAD_SHIFT = int(math.log2(HEAD_DIM))
DECAY_LORA = 64
AAA_LORA = 64
GATE_LORA = 128
POOL_WINDOWS = (2, 4, 8, 16)
POOL_WIDTH = 512
POOL_GROUP_DIM = POOL_WIDTH // len(POOL_WINDOWS)
MAX_WINDOW = max(POOL_WINDOWS)
D_FF = 4 * D_MODEL
RMS_EPS = 1e-6
GN_EPS = 64e-5
L2_EPS = 1e-12
RWKV_COLS = 3 * RWKV_WIDTH + DECAY_LORA + AAA_LORA + GATE_LORA
GATE_COLS = 2 * D_MODEL
D_IN = RWKV_COLS + POOL_WIDTH + GATE_COLS
LORA_OFF = 3 * RWKV_WIDTH
GATE_LORA_OFF = LORA_OFF + DECAY_LORA + AAA_LORA
GATE_OFF = RWKV_COLS + POOL_WIDTH

LANES = 128
SUBLANES = 8
HEADS_PER_BLOCK = LANES // HEAD_DIM
N_PAIRS = RWKV_WIDTH // LANES
HSUM_WIDTH = 256
CHUNK = 64
CHUNKS_PER_STEP = 8
RECUR_ROWS = CHUNK * CHUNKS_PER_STEP
N_DOUBLINGS = int(math.log2(CHUNK)) - 1
LOG_DECAY_SCALE = -math.exp(-0.5)

FRONT_ROWS = 512
PROJ_PIECE = 512
TAIL_ROWS = 512
VMEM_LIMIT_BYTES = 56 * 1024 * 1024


def _head_of(channel):
    return lax.shift_right_logical(channel, HEAD_SHIFT)


def _mm(a, b):
    return jnp.dot(a.astype(BF16), b.astype(BF16), preferred_element_type=F32)


def _mm_nt(a, b):
    return lax.dot_general(a.astype(BF16), b.astype(BF16), (((1,), (1,)), ((), ())),
                           preferred_element_type=F32)


def _mm_tn(a, b):
    return lax.dot_general(a.astype(BF16), b.astype(BF16), (((0,), (0,)), ((), ())),
                           preferred_element_type=F32)


def _chunk_cumsum(tri2, x):
    hi = x.astype(BF16)
    lo = (x - hi.astype(F32)).astype(BF16)
    return jnp.dot(tri2, jnp.concatenate([hi, lo], axis=0), preferred_element_type=F32)


def _head_sum(y, hsum):
    return jnp.concatenate(
        [_mm(y[:, i:i + HSUM_WIDTH], hsum) for i in range(0, RWKV_WIDTH, HSUM_WIDTH)], axis=1)


def _rmsnorm(x, g):
    return x * lax.rsqrt(jnp.mean(x * x, axis=-1, keepdims=True) + RMS_EPS) * g


def _pool_group(gi, zp_ref, pbuf, t, rows):
    win = POOL_WINDOWS[gi]
    cols = slice(gi * POOL_GROUP_DIM, (gi + 1) * POOL_GROUP_DIM)
    tok = zp_ref[:, cols]
    acc = tok
    for j in range(1, win):
        acc = acc + pbuf[MAX_WINDOW - j:MAX_WINDOW - j + rows, cols]
    pos = t * rows + lax.broadcasted_iota(jnp.int32, (rows, 1), 0)
    count = jnp.minimum(pos + 1, win).astype(F32)
    return acc / count - tok


def _front_kernel(x_ref, g_ref, w_ref, mu_ref, w0_ref, wd_ref, a0_ref, wa_ref, wg_ref,
                  kk_ref, ka_ref, rk_ref, hsum_ref, tri_ref,
                  rt_ref, at_ref, bt_ref, kt_ref, v_ref, btail_ref, ktail_ref, pc_ref,
                  bonus_ref, rgate_ref, zp_ref, zg_ref, zbuf, *, rows, tiles_per_seq):
    s = pl.program_id(0)
    t_lag = jnp.maximum(s - 1, 0) % tiles_per_seq

    @pl.when(s == 0)
    def _():
        zbuf[...] = jnp.zeros(zbuf.shape, F32)

    @pl.when(t_lag == 0)
    def _():
        zbuf[0:SUBLANES, :] = jnp.zeros((SUBLANES, RWKV_COLS), F32)

    z = zbuf[SUBLANES:SUBLANES + rows, :]
    z_prev = zbuf[SUBLANES - 1:SUBLANES - 1 + rows, :]
    zs = z + (z_prev - z) * mu_ref[...]
    r = zs[:, 0:RWKV_WIDTH]
    k = zs[:, RWKV_WIDTH:2 * RWKV_WIDTH]
    v = zs[:, 2 * RWKV_WIDTH:3 * RWKV_WIDTH]
    x_lora = zs[:, LORA_OFF:LORA_OFF + LANES]
    xg = zs[:, GATE_LORA_OFF:GATE_LORA_OFF + GATE_LORA]

    h = _rmsnorm(x_ref[...], g_ref[...]).astype(BF16)
    zbuf[0:SUBLANES, :] = z[rows - SUBLANES:, :]

    def project(first, last):
        y = jnp.dot(h, w_ref[:, first:last], preferred_element_type=F32)
        if last <= RWKV_COLS:
            zbuf[SUBLANES:SUBLANES + rows, first:last] = y
        elif last <= GATE_OFF:
            zp_ref[:, first - RWKV_COLS:last - RWKV_COLS] = y
        else:
            zg_ref[:, first - GATE_OFF:last - GATE_OFF] = y

    pieces = iter([(c, min(c + PROJ_PIECE, RWKV_COLS)) for c in range(0, RWKV_COLS, PROJ_PIECE)]
                  + [(c, c + PROJ_PIECE) for c in range(RWKV_COLS, D_IN, PROJ_PIECE)])

    def project_next(n):
        for _ in range(n):
            project(*next(pieces))

    hsum = hsum_ref[...]
    project_next(1)
    dw = _mm(jnp.tanh(x_lora), wd_ref[...])
    da = _mm(x_lora, wa_ref[...])
    rgate_ref[...] = _mm(jax.nn.sigmoid(xg), wg_ref[...])
    kk = k * kk_ref[...]
    ss = _head_sum(kk * kk, hsum)
    project_next(1)
    logw = LOG_DECAY_SCALE * jax.nn.sigmoid(w0_ref[...] + dw)
    a = jax.nn.sigmoid(a0_ref[...] + da)
    k = k * (1.0 + (a - 1.0) * ka_ref[...])
    n_chunks = rows // CHUNK
    tri2 = tri_ref[...]
    cums = [_chunk_cumsum(tri2, logw[c * CHUNK:(c + 1) * CHUNK, :]) for c in range(n_chunks)]
    bonus_sum = _head_sum(r * k * rk_ref[...], hsum)
    project_next(2)

    kk = kk * jnp.minimum(lax.rsqrt(ss), 1.0 / L2_EPS)
    beta = kk * a
    cum = jnp.concatenate(cums, axis=0)
    p_rows = [jnp.exp(x[CHUNK - 1:CHUNK, :]) for x in cums]
    p_full = jnp.concatenate([jnp.broadcast_to(x, (CHUNK, RWKV_WIDTH)) for x in p_rows], axis=0)
    pc_ref[...] = jnp.concatenate(
        [jnp.broadcast_to(x, (SUBLANES, RWKV_WIDTH)) for x in p_rows], axis=0)
    e_neg = jnp.exp(-cum)
    e_tail = p_full * e_neg
    rt_ref[...] = r * jnp.exp(cum)
    at_ref[...] = (-kk * jnp.exp(cum - logw)).astype(BF16)
    bt_ref[...] = (beta * e_neg).astype(BF16)
    kt_ref[...] = (k * e_neg).astype(BF16)
    btail_ref[...] = (beta * e_tail).astype(BF16)
    ktail_ref[...] = (k * e_tail).astype(BF16)
    v_ref[...] = v.astype(BF16)
    bonus_ref[...] = bonus_sum * v
    project_next(5)
    assert next(pieces, None) is None


def _front(x2d, batch, seq, params):
    assert (D_IN - RWKV_COLS) % PROJ_PIECE == 0
    tiles_per_seq = seq // FRONT_ROWS
    n_tiles = batch * tiles_per_seq
    m = batch * seq
    const = lambda arr: pl.BlockSpec(arr.shape, lambda s: (0,) * arr.ndim,
                                     pipeline_mode=pl.Buffered(1))
    cur = lambda s: (jnp.minimum(s, n_tiles - 1), 0)
    lag = lambda s: (jnp.maximum(s - 1, 0), 0)
    lagged = lambda width: pl.BlockSpec((FRONT_ROWS, width), lag)
    pc_rows = FRONT_ROWS // CHUNK * SUBLANES
    f32 = lambda r, w: jax.ShapeDtypeStruct((r, w), F32)
    bf16 = lambda r, w: jax.ShapeDtypeStruct((r, w), BF16)
    w = RWKV_WIDTH
    return pl.pallas_call(
        functools.partial(_front_kernel, rows=FRONT_ROWS, tiles_per_seq=tiles_per_seq),
        grid=(n_tiles + 1,),
        in_specs=[pl.BlockSpec((FRONT_ROWS, D_MODEL), cur)] + [const(p) for p in params],
        out_specs=[lagged(w)] * 7 + [pl.BlockSpec((pc_rows, w), lag)]
                  + [lagged(w), lagged(w),
                     pl.BlockSpec((FRONT_ROWS, POOL_WIDTH), cur),
                     pl.BlockSpec((FRONT_ROWS, GATE_COLS), cur)],
        out_shape=[f32(m, w)] + [bf16(m, w)] * 6 + [f32(m // CHUNK * SUBLANES, w)]
                  + [f32(m, w), f32(m, w), f32(m, POOL_WIDTH), f32(m, GATE_COLS)],
        scratch_shapes=[pltpu.VMEM((FRONT_ROWS + SUBLANES, RWKV_COLS), F32)],
        compiler_params=pltpu.CompilerParams(dimension_semantics=("arbitrary",),
                                             vmem_limit_bytes=VMEM_LIMIT_BYTES),
        name="front",
    )(x2d, *params)


def _recur_kernel(rt_ref, at_ref, bt_ref, kt_ref, v_ref, btail_ref, ktail_ref, pc_ref, bonus_ref,
                  rgate_ref, zp_ref, lnw_ref, lnb_ref, hsum_ref, poolw_ref, pscale_ref,
                  out_ref, yb_ref, g_state, pbuf):
    t = pl.program_id(1)

    @pl.when(t == 0)
    def _():
        g_state[...] = jnp.zeros(g_state.shape, F32)
        pbuf[0:MAX_WINDOW, :] = jnp.zeros((MAX_WINDOW, POOL_WIDTH), F32)

    pbuf[MAX_WINDOW:MAX_WINDOW + RECUR_ROWS, :] = zp_ref[...]
    pooled = []


    row = lax.broadcasted_iota(jnp.int32, (CHUNK, LANES), 0)
    col = lax.broadcasted_iota(jnp.int32, (CHUNK, LANES), 1) & (HEAD_DIM - 1)
    strict_lower = col < row
    lower = col <= row
    eye = (col == row).astype(F32)
    lane = lax.broadcasted_iota(jnp.int32, (1, LANES), 1)
    head_masks = [(_head_of(lane) == h).astype(BF16) for h in range(HEADS_PER_BLOCK)]
    head_masks2 = [jnp.concatenate([m, m], axis=1) for m in head_masks]
    brow = lax.broadcasted_iota(jnp.int32, (LANES, LANES), 0)
    bcol = lax.broadcasted_iota(jnp.int32, (LANES, LANES), 1)
    same_head = _head_of(brow) == _head_of(bcol)
    diag = brow == bcol

    def block_diag(x):
        xb = x.astype(BF16)
        masks = head_masks if x.shape[1] == LANES else head_masks2
        return jnp.concatenate([xb * m for m in masks], axis=0)

    stack = lambda x, y: jnp.concatenate([x.astype(BF16), y.astype(BF16)], axis=0)
    no_w2 = jnp.zeros((CHUNK, LANES), BF16)

    def block_levels(chunks, out, with_pooling):
        blocks = [(slice(c * CHUNK, (c + 1) * CHUNK), slice(p * LANES, (p + 1) * LANES))
                  for c in chunks for p in range(N_PAIRS)]
        a_ps = [at_ref[blk] for blk in blocks]
        r_ps = [rt_ref[blk] for blk in blocks]
        v_ps = [v_ref[blk] for blk in blocks]
        v_bd = [block_diag(x) for x in v_ps]
        tri4 = [_mm_nt(stack(x, y),
                       jnp.concatenate([block_diag(bt_ref[blk]), block_diag(kt_ref[blk])], axis=0))
                for x, y, blk in zip(a_ps, r_ps, blocks)]
        a_ab = [jnp.where(strict_lower, x[:CHUNK, :LANES], 0.0) for x in tri4]
        a_ak = [jnp.where(strict_lower, x[:CHUNK, LANES:], 0.0) for x in tri4]
        a_rb = [jnp.where(lower, x[CHUNK:, :LANES], 0.0) for x in tri4]
        a_rk = [jnp.where(lower, x[CHUNK:, LANES:], 0.0) for x in tri4]
        yield
        xk = [_mm(stack(x, y), z) for x, y, z in zip(a_ak, a_rk, v_bd)]
        x1 = [x[:CHUNK] for x in xk]
        yk = [x[CHUNK:] for x in xk]
        t_inv = [eye + x for x in a_ab]
        pw = [_mm(x, block_diag(x)) for x in a_ab]
        yield
        for level in range(N_DOUBLINGS - 1):
            both = [_mm(stack(x, y), block_diag(x)) for x, y in zip(pw, t_inv)]
            pw = [x[:CHUNK] for x in both]
            t_inv = [x + y[CHUNK:] for x, y in zip(t_inv, both)]
            if with_pooling and level < len(POOL_WINDOWS):
                pooled.append(_mm(_pool_group(level, zp_ref, pbuf, t, RECUR_ROWS), poolw_ref[level]))
            yield
        t_inv = [x + _mm(x, block_diag(y)) for x, y in zip(t_inv, pw)]
        yield
        w12 = [_mm(x, block_diag(jnp.concatenate([y.astype(BF16), z], axis=1)))
               for x, y, z in zip(t_inv, x1, a_ps)]
        yield
        p_last = [pc_ref[c * SUBLANES:c * SUBLANES + 1, p * LANES:(p + 1) * LANES]
                  for c in chunks for p in range(N_PAIRS)]
        mn = [_mm_tn(stack(btail_ref[blk], ktail_ref[blk]),
                     jnp.concatenate([jnp.concatenate([w12[q][:, LANES:], w12[q][:, :LANES]], axis=1)
                                      .astype(BF16), jnp.concatenate([no_w2, v_ps[q]], axis=1)], axis=0))
              for q, blk in enumerate(blocks)]
        out["m_t"] = [jnp.where(same_head, x[:, :LANES], 0.0) + jnp.where(diag, p_last[q], 0.0)
                      for q, x in enumerate(mn)]
        out["n_t"] = [jnp.where(same_head, x[:, LANES:], 0.0) for x in mn]
        out.update(a_rb=a_rb, w12=w12, yk=yk, r_ps=r_ps)
        yield

    assert N_DOUBLINGS - 1 >= len(POOL_WINDOWS)
    half = CHUNKS_PER_STEP // 2
    groups = [range(0, half), range(half, CHUNKS_PER_STEP)]
    res = [{}, {}]
    states = [[g_state[p] for p in range(N_PAIRS)]]
    yr = []

    def chain_step(c):
        g = res[c // half]
        qs = range((c % half) * N_PAIRS, (c % half + 1) * N_PAIRS)
        states.append([_mm(g["m_t"][q], states[c][p]) + g["n_t"][q] for p, q in enumerate(qs)])
        yr.extend(_mm(g["a_rb"][q], block_diag(g["w12"][q])) for q in qs)

    for _ in block_levels(groups[0], res[0], True):
        pass
    pbuf[0:MAX_WINDOW, :] = zp_ref[RECUR_ROWS - MAX_WINDOW:, :]
    pending = list(groups[0])
    for _ in block_levels(groups[1], res[1], False):
        if pending:
            chain_step(pending.pop(0))
    for c in pending + list(groups[1]):
        chain_step(c)
    for p in range(N_PAIRS):
        g_state[p] = states[CHUNKS_PER_STEP][p]
    yk = res[0]["yk"] + res[1]["yk"]
    r_ps = res[0]["r_ps"] + res[1]["r_ps"]
    ys = [yk[q] + yr[q][:, :LANES] + _mm(r_ps[q] + yr[q][:, LANES:], states[q // N_PAIRS][q % N_PAIRS])
          for q in range(CHUNKS_PER_STEP * N_PAIRS)]
    y = jnp.concatenate([jnp.concatenate(ys[c * N_PAIRS:(c + 1) * N_PAIRS], axis=1)
                         for c in range(CHUNKS_PER_STEP)], axis=0)

    yb_ref[...] = jnp.concatenate(pooled, axis=1) * pscale_ref[...]

    hsum = hsum_ref[...]
    mean = _head_sum(y, hsum) * (1.0 / HEAD_DIM)
    d = y - mean
    var = _head_sum(d * d, hsum) * (1.0 / HEAD_DIM)
    yn = d * lax.rsqrt(var + GN_EPS) * lnw_ref[...] + lnb_ref[...]
    out_ref[...] = (yn + bonus_ref[...]) * rgate_ref[...]


def _recur(rt, at, bt, kt, v, btail, ktail, pc, bonus, rgate, zp, batch, seq, params):
    assert RWKV_WIDTH == POOL_WIDTH
    n_t = seq // RECUR_ROWS
    const = lambda arr: pl.BlockSpec(arr.shape, lambda b, t: (0,) * arr.ndim,
                                     pipeline_mode=pl.Buffered(1))
    rows = pl.BlockSpec((RECUR_ROWS, RWKV_WIDTH), lambda b, t: (b * n_t + t, 0))
    pc_rows = pl.BlockSpec((CHUNKS_PER_STEP * SUBLANES, RWKV_WIDTH), lambda b, t: (b * n_t + t, 0))
    out = jax.ShapeDtypeStruct((batch * seq, RWKV_WIDTH), F32)
    return pl.pallas_call(
        _recur_kernel,
        grid=(batch, n_t),
        in_specs=[rows] * 7 + [pc_rows, rows, rows, rows] + [const(p) for p in params],
        out_specs=[rows, rows],
        out_shape=[out, out],
        scratch_shapes=[pltpu.VMEM((N_PAIRS, LANES, LANES), F32),
                        pltpu.VMEM((RECUR_ROWS + MAX_WINDOW, POOL_WIDTH), F32)],
        compiler_params=pltpu.CompilerParams(dimension_semantics=("arbitrary", "arbitrary"),
                                             vmem_limit_bytes=VMEM_LIMIT_BYTES),
        name="recur",
    )(rt, at, bt, kt, v, btail, ktail, pc, bonus, rgate, zp, *params)


def _tail_kernel(x_ref, ya_ref, yb_ref, zg_ref, p_ref, bg_ref, woa_ref, wob_ref,
                 wo_ref, gmlp_ref, w1_ref, w2_ref, gple_ref, wpg_ref, wpe_ref, gfin_ref, out_ref):
    y_a = _mm(ya_ref[...], woa_ref[...])
    y_b = _mm(yb_ref[...], wob_ref[...])
    ple = _mm(p_ref[...], wpe_ref[...])
    gate = jax.nn.sigmoid(zg_ref[...] + bg_ref[...])
    merged = gate[:, :D_MODEL] * y_a + gate[:, D_MODEL:] * y_b
    x = x_ref[...] + _mm(merged, wo_ref[...])

    f = jnp.maximum(_mm(_rmsnorm(x, gmlp_ref[...]), w1_ref[...]), 0.0)
    x = x + _mm(f * f, w2_ref[...])

    h = _rmsnorm(x, gple_ref[...])
    x = x + jax.nn.sigmoid(_mm(h, wpg_ref[...])) * ple
    out_ref[...] = _rmsnorm(x, gfin_ref[...])


def _tail(x2d, ya, yb, zg, p2d, params):
    m = x2d.shape[0]
    rows = lambda width: pl.BlockSpec((TAIL_ROWS, width), lambda i: (i, 0))
    const = lambda arr: pl.BlockSpec(arr.shape, lambda i: (0,) * arr.ndim,
                                     pipeline_mode=pl.Buffered(1))
    return pl.pallas_call(
        _tail_kernel,
        grid=(m // TAIL_ROWS,),
        in_specs=[rows(D_MODEL), rows(RWKV_WIDTH), rows(POOL_WIDTH), rows(GATE_COLS), rows(PLE_DIM)]
                 + [const(p) for p in params],
        out_specs=rows(D_MODEL),
        out_shape=jax.ShapeDtypeStruct((m, D_MODEL), F32),
        compiler_params=pltpu.CompilerParams(dimension_semantics=("arbitrary",),
                                             vmem_limit_bytes=VMEM_LIMIT_BYTES),
        name="tail",
    )(x2d, ya, yb, zg, p2d, *params)


def _pad_rows(w, before, total):
    return jnp.pad(w, ((before, total - before - w.shape[0]), (0, 0)))


def kernel(x, p, g_mix, w_in, mu_shift, w0, w_decay_up, a0, w_aaa_up, w_gate_up, k_k, k_a, r_k, ln_x_w, ln_x_b, pool_w, pool_scale, b_gates, w_out_a, w_out_b, w_o, g_mlp, w_ff1, w_ff2, g_ple, w_ple_gate, w_ple_proj, g_final):
    batch, seq, d = x.shape
    depth = w_in.shape[0]
    assert d == D_MODEL and seq % FRONT_ROWS == 0 and seq % RECUR_ROWS == 0
    assert (batch * seq) % TAIL_ROWS == 0
    assert depth == 1, "the fused tail kernel supports a single layer"
    i = 0
    row = lambda a: a.reshape(1, -1)
    bf = lambda a: a.astype(BF16)

    head_id = jnp.arange(HSUM_WIDTH) // HEAD_DIM
    hsum = (head_id[:, None] == head_id[None, :]).astype(BF16)
    tok = jnp.arange(CHUNK)
    tri = (tok[None, :] <= tok[:, None]).astype(BF16)
    tri = jnp.concatenate([tri, tri], axis=1)

    x2d = x.reshape(batch * seq, d)
    front_params = (row(g_mix[i]), bf(w_in[i]), row(mu_shift[i]), row(w0[i]),
                    bf(_pad_rows(w_decay_up[i], 0, LANES)), row(a0[i]),
                    bf(_pad_rows(w_aaa_up[i], DECAY_LORA, LANES)), bf(w_gate_up[i]),
                    row(k_k[i]), row(k_a[i]), row(r_k[i]), hsum, tri)
    (rt, at, bt, kt, v, btail, ktail, pc, bonus, rgate, zp, zg) = _front(x2d, batch, seq,
                                                                         front_params)
    recur_params = (row(ln_x_w[i]), row(ln_x_b[i]), hsum, bf(pool_w[i]), row(pool_scale[i]))
    ya, yb = _recur(rt, at, bt, kt, v, btail, ktail, pc, bonus, rgate, zp, batch, seq, recur_params)
    tail_params = (row(b_gates[i]), bf(w_out_a[i]), bf(w_out_b[i]), bf(w_o[i]), row(g_mlp[i]),
                   bf(w_ff1[i]), bf(w_ff2[i]), row(g_ple[i]), bf(w_ple_gate[i]), bf(w_ple_proj[i]),
                   row(g_final))
    out = _tail(x2d, ya, yb, zg, p[i].reshape(batch * seq, PLE_DIM), tail_params)
    return out.reshape(batch, seq, d)
```

```python
import functools
import math

import jax
import jax.numpy as jnp
from jax import lax
from jax.experimental import pallas as pl
from jax.experimental.pallas import tpu as pltpu

F32 = jnp.float32
BF16 = jnp.bfloat16

D_MODEL = 1024
PLE_DIM = 256
HEAD_DIM = 64
RWKV_WIDTH = 512
HEAD_SHIFT = int(math.log2(HEAD_DIM))
DECAY_LORA = 64
AAA_LORA = 64
GATE_LORA = 128
POOL_WINDOWS = (2, 4, 8, 16)
POOL_WIDTH = 512
POOL_GROUP_DIM = POOL_WIDTH // len(POOL_WINDOWS)
MAX_WINDOW = max(POOL_WINDOWS)
D_FF = 4 * D_MODEL
RMS_EPS = 1e-6
GN_EPS = 64e-5
L2_EPS = 1e-12
RWKV_COLS = 3 * RWKV_WIDTH + DECAY_LORA + AAA_LORA + GATE_LORA
GATE_COLS = 2 * D_MODEL
D_IN = RWKV_COLS + POOL_WIDTH + GATE_COLS
LORA_OFF = 3 * RWKV_WIDTH
GATE_LORA_OFF = LORA_OFF + DECAY_LORA + AAA_LORA
GATE_OFF = RWKV_COLS + POOL_WIDTH

LANES = 128
SUBLANES = 8
HEADS_PER_BLOCK = LANES // HEAD_DIM
N_PAIRS = RWKV_WIDTH // LANES
HSUM_WIDTH = 256
CHUNK = 64
CHUNKS_PER_STEP = 8
RECUR_ROWS = CHUNK * CHUNKS_PER_STEP
N_DOUBLINGS = int(math.log2(CHUNK)) - 1
LOG_DECAY_SCALE = -math.exp(-0.5)

FRONT_ROWS = 512
PROJ_PIECE = 512
TAIL_ROWS = 512
N_TAIL_CASTS = 4
VMEM_LIMIT_BYTES = 56 * 1024 * 1024


def _head_of(channel):
    return lax.shift_right_logical(channel, HEAD_SHIFT)


def _mm(a, b):
    return jnp.dot(a.astype(BF16), b.astype(BF16), preferred_element_type=F32)


def _mm_nt(a, b):
    return lax.dot_general(a.astype(BF16), b.astype(BF16), (((1,), (1,)), ((), ())),
                           preferred_element_type=F32)


def _mm_tn(a, b):
    return lax.dot_general(a.astype(BF16), b.astype(BF16), (((0,), (0,)), ((), ())),
                           preferred_element_type=F32)


def _chunk_cumsum(tri2, x):
    hi = x.astype(BF16)
    lo = (x - hi.astype(F32)).astype(BF16)
    return jnp.dot(tri2, jnp.concatenate([hi, lo], axis=0), preferred_element_type=F32)


def _head_sum(y, hsum):
    return jnp.concatenate(
        [_mm(y[:, i:i + HSUM_WIDTH], hsum) for i in range(0, RWKV_WIDTH, HSUM_WIDTH)], axis=1)


def _rmsnorm(x, g):
    return x * lax.rsqrt(jnp.mean(x * x, axis=-1, keepdims=True) + RMS_EPS) * g


def _pool_group(gi, zp_ref, pbuf, t, rows):
    win = POOL_WINDOWS[gi]
    cols = slice(gi * POOL_GROUP_DIM, (gi + 1) * POOL_GROUP_DIM)
    tok = zp_ref[:, cols]
    acc = tok
    for j in range(1, win):
        acc = acc + pbuf[MAX_WINDOW - j:MAX_WINDOW - j + rows, cols]
    pos = t * rows + lax.broadcasted_iota(jnp.int32, (rows, 1), 0)
    count = jnp.minimum(pos + 1, win).astype(F32)
    return acc / count - tok


def _front_kernel(x_ref, g_ref, w_ref, mu_ref, w0_ref, wd_ref, a0_ref, wa_ref, wg_ref,
                  kk_ref, ka_ref, rk_ref, hsum_ref, tri_ref,
                  rt_ref, at_ref, bt_ref, kt_ref, v_ref, btail_ref, ktail_ref, pc_ref,
                  bonus_ref, rgate_ref, zp_ref, zg_ref, zbuf, *, rows, tiles_per_seq):
    s = pl.program_id(0)
    t_lag = jnp.maximum(s - 1, 0) % tiles_per_seq

    @pl.when(s == 0)
    def _():
        zbuf[...] = jnp.zeros(zbuf.shape, F32)

    @pl.when(t_lag == 0)
    def _():
        zbuf[0:SUBLANES, :] = jnp.zeros((SUBLANES, RWKV_COLS), F32)

    z = zbuf[SUBLANES:SUBLANES + rows, :]
    z_prev = zbuf[SUBLANES - 1:SUBLANES - 1 + rows, :]
    zs = z + (z_prev - z) * mu_ref[...]
    r = zs[:, 0:RWKV_WIDTH]
    k = zs[:, RWKV_WIDTH:2 * RWKV_WIDTH]
    v = zs[:, 2 * RWKV_WIDTH:3 * RWKV_WIDTH]
    x_lora = zs[:, LORA_OFF:LORA_OFF + LANES]
    xg = zs[:, GATE_LORA_OFF:GATE_LORA_OFF + GATE_LORA]

    h = _rmsnorm(x_ref[...], g_ref[...]).astype(BF16)
    zbuf[0:SUBLANES, :] = z[rows - SUBLANES:, :]

    def project(first, last):
        y = jnp.dot(h, w_ref[:, first:last], preferred_element_type=F32)
        if last <= RWKV_COLS:
            zbuf[SUBLANES:SUBLANES + rows, first:last] = y
        elif last <= GATE_OFF:
            zp_ref[:, first - RWKV_COLS:last - RWKV_COLS] = y
        else:
            zg_ref[:, first - GATE_OFF:last - GATE_OFF] = y

    pieces = iter([(c, min(c + PROJ_PIECE, RWKV_COLS)) for c in range(0, RWKV_COLS, PROJ_PIECE)]
                  + [(c, c + PROJ_PIECE) for c in range(RWKV_COLS, D_IN, PROJ_PIECE)])

    def project_next(n):
        for _ in range(n):
            project(*next(pieces))

    hsum = hsum_ref[...]
    project_next(1)
    dw = _mm(jnp.tanh(x_lora), wd_ref[...])
    da = _mm(x_lora, wa_ref[...])
    rgate_ref[...] = _mm(jax.nn.sigmoid(xg), wg_ref[...])
    kk = k * kk_ref[...]
    ss = _head_sum(kk * kk, hsum)
    project_next(1)
    logw = LOG_DECAY_SCALE * jax.nn.sigmoid(w0_ref[...] + dw)
    a = jax.nn.sigmoid(a0_ref[...] + da)
    k = k * (1.0 + (a - 1.0) * ka_ref[...])
    n_chunks = rows // CHUNK
    tri2 = tri_ref[...]
    cums = [_chunk_cumsum(tri2, logw[c * CHUNK:(c + 1) * CHUNK, :]) for c in range(n_chunks)]
    bonus_sum = _head_sum(r * k * rk_ref[...], hsum)
    project_next(2)

    kk = kk * jnp.minimum(lax.rsqrt(ss), 1.0 / L2_EPS)
    beta = kk * a
    cum = jnp.concatenate(cums, axis=0)
    p_rows = [jnp.exp(x[CHUNK - 1:CHUNK, :]) for x in cums]
    p_full = jnp.concatenate([jnp.broadcast_to(x, (CHUNK, RWKV_WIDTH)) for x in p_rows], axis=0)
    pc_ref[...] = jnp.concatenate(
        [jnp.broadcast_to(x, (SUBLANES, RWKV_WIDTH)) for x in p_rows], axis=0)
    e_neg = jnp.exp(-cum)
    e_tail = p_full * e_neg
    rt_ref[...] = r * jnp.exp(cum)
    at_ref[...] = (-kk * jnp.exp(cum - logw)).astype(BF16)
    bt_ref[...] = (beta * e_neg).astype(BF16)
    kt_ref[...] = (k * e_neg).astype(BF16)
    btail_ref[...] = (beta * e_tail).astype(BF16)
    ktail_ref[...] = (k * e_tail).astype(BF16)
    v_ref[...] = v.astype(BF16)
    bonus_ref[...] = bonus_sum * v
    project_next(5)
    assert next(pieces, None) is None


def _front(x2d, batch, seq, params):
    assert (D_IN - RWKV_COLS) % PROJ_PIECE == 0
    tiles_per_seq = seq // FRONT_ROWS
    n_tiles = batch * tiles_per_seq
    m = batch * seq
    const = lambda arr: pl.BlockSpec(arr.shape, lambda s: (0,) * arr.ndim,
                                     pipeline_mode=pl.Buffered(1))
    cur = lambda s: (jnp.minimum(s, n_tiles - 1), 0)
    lag = lambda s: (jnp.maximum(s - 1, 0), 0)
    lagged = lambda width: pl.BlockSpec((FRONT_ROWS, width), lag)
    pc_rows = FRONT_ROWS // CHUNK * SUBLANES
    f32 = lambda r, w: jax.ShapeDtypeStruct((r, w), F32)
    bf16 = lambda r, w: jax.ShapeDtypeStruct((r, w), BF16)
    w = RWKV_WIDTH
    return pl.pallas_call(
        functools.partial(_front_kernel, rows=FRONT_ROWS, tiles_per_seq=tiles_per_seq),
        grid=(n_tiles + 1,),
        in_specs=[pl.BlockSpec((FRONT_ROWS, D_MODEL), cur)] + [const(p) for p in params],
        out_specs=[lagged(w)] * 7 + [pl.BlockSpec((pc_rows, w), lag)]
                  + [lagged(w), lagged(w),
                     pl.BlockSpec((FRONT_ROWS, POOL_WIDTH), cur),
                     pl.BlockSpec((FRONT_ROWS, GATE_COLS), cur)],
        out_shape=[f32(m, w)] + [bf16(m, w)] * 6 + [f32(m // CHUNK * SUBLANES, w)]
                  + [f32(m, w), f32(m, w), f32(m, POOL_WIDTH), f32(m, GATE_COLS)],
        scratch_shapes=[pltpu.VMEM((FRONT_ROWS + SUBLANES, RWKV_COLS), F32)],
        compiler_params=pltpu.CompilerParams(dimension_semantics=("arbitrary",),
                                             vmem_limit_bytes=VMEM_LIMIT_BYTES),
        name="front",
    )(x2d, *params)


def _recur_kernel(rt_ref, at_ref, bt_ref, kt_ref, v_ref, btail_ref, ktail_ref, pc_ref, bonus_ref,
                  rgate_ref, zp_ref, lnw_ref, lnb_ref, hsum_ref, poolw_ref, pscale_ref, *rest):
    n_cast = N_TAIL_CASTS
    cast_in, (out_ref, yb_ref) = rest[:n_cast], rest[n_cast:n_cast + 2]
    cast_out, (g_state, pbuf) = rest[n_cast + 2:2 * n_cast + 2], rest[2 * n_cast + 2:]
    t = pl.program_id(1)

    for src, dst in zip(cast_in, cast_out):
        dst[...] = src[...].astype(BF16)

    @pl.when(t == 0)
    def _():
        g_state[...] = jnp.zeros(g_state.shape, F32)
        pbuf[0:MAX_WINDOW, :] = jnp.zeros((MAX_WINDOW, POOL_WIDTH), F32)

    pbuf[MAX_WINDOW:MAX_WINDOW + RECUR_ROWS, :] = zp_ref[...]
    pooled = []


    row = lax.broadcasted_iota(jnp.int32, (CHUNK, LANES), 0)
    col = lax.broadcasted_iota(jnp.int32, (CHUNK, LANES), 1) & (HEAD_DIM - 1)
    strict_lower = col < row
    lower = col <= row
    eye = (col == row).astype(F32)
    lane = lax.broadcasted_iota(jnp.int32, (1, LANES), 1)
    head_masks = [(_head_of(lane) == h).astype(BF16) for h in range(HEADS_PER_BLOCK)]
    head_masks2 = [jnp.concatenate([m, m], axis=1) for m in head_masks]
    brow = lax.broadcasted_iota(jnp.int32, (LANES, LANES), 0)
    bcol = lax.broadcasted_iota(jnp.int32, (LANES, LANES), 1)
    same_head = _head_of(brow) == _head_of(bcol)
    diag = brow == bcol

    def block_diag(x):
        xb = x.astype(BF16)
        masks = head_masks if x.shape[1] == LANES else head_masks2
        return jnp.concatenate([xb * m for m in masks], axis=0)

    stack = lambda x, y: jnp.concatenate([x.astype(BF16), y.astype(BF16)], axis=0)
    no_w2 = jnp.zeros((CHUNK, LANES), BF16)

    def block_levels(chunks, out, with_pooling):
        blocks = [(slice(c * CHUNK, (c + 1) * CHUNK), slice(p * LANES, (p + 1) * LANES))
                  for c in chunks for p in range(N_PAIRS)]
        a_ps = [at_ref[blk] for blk in blocks]
        r_ps = [rt_ref[blk] for blk in blocks]
        v_ps = [v_ref[blk] for blk in blocks]
        v_bd = [block_diag(x) for x in v_ps]
        tri4 = [_mm_nt(stack(x, y),
                       jnp.concatenate([block_diag(bt_ref[blk]), block_diag(kt_ref[blk])], axis=0))
                for x, y, blk in zip(a_ps, r_ps, blocks)]
        a_ab = [jnp.where(strict_lower, x[:CHUNK, :LANES], 0.0) for x in tri4]
        a_ak = [jnp.where(strict_lower, x[:CHUNK, LANES:], 0.0) for x in tri4]
        a_rb = [jnp.where(lower, x[CHUNK:, :LANES], 0.0) for x in tri4]
        a_rk = [jnp.where(lower, x[CHUNK:, LANES:], 0.0) for x in tri4]
        yield
        xk = [_mm(stack(x, y), z) for x, y, z in zip(a_ak, a_rk, v_bd)]
        x1 = [x[:CHUNK] for x in xk]
        yk = [x[CHUNK:] for x in xk]
        t_inv = [eye + x for x in a_ab]
        pw = [_mm(x, block_diag(x)) for x in a_ab]
        yield
        for level in range(N_DOUBLINGS - 1):
            both = [_mm(stack(x, y), block_diag(x)) for x, y in zip(pw, t_inv)]
            pw = [x[:CHUNK] for x in both]
            t_inv = [x + y[CHUNK:] for x, y in zip(t_inv, both)]
            if with_pooling and level < len(POOL_WINDOWS):
                pooled.append(_mm(_pool_group(level, zp_ref, pbuf, t, RECUR_ROWS), poolw_ref[level]))
            yield
        t_inv = [x + _mm(x, block_diag(y)) for x, y in zip(t_inv, pw)]
        yield
        w12 = [_mm(x, block_diag(jnp.concatenate([y.astype(BF16), z], axis=1)))
               for x, y, z in zip(t_inv, x1, a_ps)]
        yield
        p_last = [pc_ref[c * SUBLANES:c * SUBLANES + 1, p * LANES:(p + 1) * LANES]
                  for c in chunks for p in range(N_PAIRS)]
        mn = [_mm_tn(stack(btail_ref[blk], ktail_ref[blk]),
                     jnp.concatenate([jnp.concatenate([w12[q][:, LANES:], w12[q][:, :LANES]], axis=1)
                                      .astype(BF16), jnp.concatenate([no_w2, v_ps[q]], axis=1)], axis=0))
              for q, blk in enumerate(blocks)]
        out["m_t"] = [jnp.where(same_head, x[:, :LANES], 0.0) + jnp.where(diag, p_last[q], 0.0)
                      for q, x in enumerate(mn)]
        out["n_t"] = [jnp.where(same_head, x[:, LANES:], 0.0) for x in mn]
        out.update(a_rb=a_rb, w12=w12, yk=yk, r_ps=r_ps)
        yield

    assert N_DOUBLINGS - 1 >= len(POOL_WINDOWS)
    half = CHUNKS_PER_STEP // 2
    groups = [range(0, half), range(half, CHUNKS_PER_STEP)]
    res = [{}, {}]
    states = [[g_state[p] for p in range(N_PAIRS)]]
    yr = []

    def chain_step(c):
        g = res[c // half]
        qs = range((c % half) * N_PAIRS, (c % half + 1) * N_PAIRS)
        states.append([_mm(g["m_t"][q], states[c][p]) + g["n_t"][q] for p, q in enumerate(qs)])
        yr.extend(_mm(g["a_rb"][q], block_diag(g["w12"][q])) for q in qs)

    for _ in block_levels(groups[0], res[0], True):
        pass
    pbuf[0:MAX_WINDOW, :] = zp_ref[RECUR_ROWS - MAX_WINDOW:, :]
    pending = list(groups[0])
    for _ in block_levels(groups[1], res[1], False):
        if pending:
            chain_step(pending.pop(0))
    for c in pending + list(groups[1]):
        chain_step(c)
    for p in range(N_PAIRS):
        g_state[p] = states[CHUNKS_PER_STEP][p]
    yk = res[0]["yk"] + res[1]["yk"]
    r_ps = res[0]["r_ps"] + res[1]["r_ps"]
    ys = [yk[q] + yr[q][:, :LANES] + _mm(r_ps[q] + yr[q][:, LANES:], states[q // N_PAIRS][q % N_PAIRS])
          for q in range(CHUNKS_PER_STEP * N_PAIRS)]
    y = jnp.concatenate([jnp.concatenate(ys[c * N_PAIRS:(c + 1) * N_PAIRS], axis=1)
                         for c in range(CHUNKS_PER_STEP)], axis=0)

    yb_ref[...] = jnp.concatenate(pooled, axis=1) * pscale_ref[...]

    hsum = hsum_ref[...]
    mean = _head_sum(y, hsum) * (1.0 / HEAD_DIM)
    d = y - mean
    var = _head_sum(d * d, hsum) * (1.0 / HEAD_DIM)
    yn = d * lax.rsqrt(var + GN_EPS) * lnw_ref[...] + lnb_ref[...]
    out_ref[...] = (yn + bonus_ref[...]) * rgate_ref[...]


def _recur(rt, at, bt, kt, v, btail, ktail, pc, bonus, rgate, zp, batch, seq, params, tail_weights):
    assert RWKV_WIDTH == POOL_WIDTH
    assert len(tail_weights) == N_TAIL_CASTS
    n_t = seq // RECUR_ROWS
    n_steps = batch * n_t
    step = lambda b, t: (b * n_t + t, 0)
    const = lambda arr: pl.BlockSpec(arr.shape, lambda b, t: (0,) * arr.ndim,
                                     pipeline_mode=pl.Buffered(1))
    rows = pl.BlockSpec((RECUR_ROWS, RWKV_WIDTH), step)
    pc_rows = pl.BlockSpec((CHUNKS_PER_STEP * SUBLANES, RWKV_WIDTH), step)
    out = jax.ShapeDtypeStruct((batch * seq, RWKV_WIDTH), F32)
    for w in tail_weights:
        assert w.shape[0] % (n_steps * 2 * SUBLANES) == 0, w.shape
    w_slices = [pl.BlockSpec((w.shape[0] // n_steps, w.shape[1]), step) for w in tail_weights]
    results = pl.pallas_call(
        _recur_kernel,
        grid=(batch, n_t),
        in_specs=[rows] * 7 + [pc_rows, rows, rows, rows] + [const(p) for p in params] + w_slices,
        out_specs=[rows, rows] + w_slices,
        out_shape=[out, out] + [jax.ShapeDtypeStruct(w.shape, BF16) for w in tail_weights],
        scratch_shapes=[pltpu.VMEM((N_PAIRS, LANES, LANES), F32),
                        pltpu.VMEM((RECUR_ROWS + MAX_WINDOW, POOL_WIDTH), F32)],
        compiler_params=pltpu.CompilerParams(dimension_semantics=("arbitrary", "arbitrary"),
                                             vmem_limit_bytes=VMEM_LIMIT_BYTES),
        name="recur",
    )(rt, at, bt, kt, v, btail, ktail, pc, bonus, rgate, zp, *params, *tail_weights)
    return results[0], results[1], results[2:]


def _tail_kernel(x_ref, ya_ref, yb_ref, zg_ref, p_ref, bg_ref, woa_ref, wob_ref,
                 wo_ref, gmlp_ref, w1_ref, w2_ref, gple_ref, wpg_ref, wpe_ref, gfin_ref, out_ref):
    y_a = _mm(ya_ref[...], woa_ref[...])
    y_b = _mm(yb_ref[...], wob_ref[...])
    ple = _mm(p_ref[...], wpe_ref[...])
    gate = jax.nn.sigmoid(zg_ref[...] + bg_ref[...])
    merged = gate[:, :D_MODEL] * y_a + gate[:, D_MODEL:] * y_b
    x = x_ref[...] + _mm(merged, wo_ref[...])

    f = jnp.maximum(_mm(_rmsnorm(x, gmlp_ref[...]), w1_ref[...]), 0.0)
    x = x + _mm(f * f, w2_ref[...])

    n = x.shape[0]
    halves = (slice(0, n // 2), slice(n // 2, n))
    gate_pre = [_mm(_rmsnorm(x[r], gple_ref[...]), wpg_ref[...]) for r in halves]
    for r, pre in zip(halves, gate_pre):
        out_ref[r, :] = _rmsnorm(x[r] + jax.nn.sigmoid(pre) * ple[r], gfin_ref[...])


def _tail(x2d, ya, yb, zg, p2d, params):
    m = x2d.shape[0]
    rows = lambda width: pl.BlockSpec((TAIL_ROWS, width), lambda i: (i, 0))
    const = lambda arr: pl.BlockSpec(arr.shape, lambda i: (0,) * arr.ndim,
                                     pipeline_mode=pl.Buffered(1))
    return pl.pallas_call(
        _tail_kernel,
        grid=(m // TAIL_ROWS,),
        in_specs=[rows(D_MODEL), rows(RWKV_WIDTH), rows(POOL_WIDTH), rows(GATE_COLS), rows(PLE_DIM)]
                 + [const(p) for p in params],
        out_specs=rows(D_MODEL),
        out_shape=jax.ShapeDtypeStruct((m, D_MODEL), F32),
        compiler_params=pltpu.CompilerParams(dimension_semantics=("arbitrary",),
                                             vmem_limit_bytes=VMEM_LIMIT_BYTES),
        name="tail",
    )(x2d, ya, yb, zg, p2d, *params)


def _pad_rows(w, before, total):
    return jnp.pad(w, ((before, total - before - w.shape[0]), (0, 0)))


def kernel(x, p, g_mix, w_in, mu_shift, w0, w_decay_up, a0, w_aaa_up, w_gate_up, k_k, k_a, r_k, ln_x_w, ln_x_b, pool_w, pool_scale, b_gates, w_out_a, w_out_b, w_o, g_mlp, w_ff1, w_ff2, g_ple, w_ple_gate, w_ple_proj, g_final):
    batch, seq, d = x.shape
    depth = w_in.shape[0]
    assert d == D_MODEL and seq % FRONT_ROWS == 0 and seq % RECUR_ROWS == 0
    assert (batch * seq) % TAIL_ROWS == 0
    assert depth == 1, "the fused tail kernel supports a single layer"
    i = 0
    row = lambda a: a.reshape(1, -1)
    bf = lambda a: a.astype(BF16)

    head_id = jnp.arange(HSUM_WIDTH) // HEAD_DIM
    hsum = (head_id[:, None] == head_id[None, :]).astype(BF16)
    tok = jnp.arange(CHUNK)
    tri = (tok[None, :] <= tok[:, None]).astype(BF16)
    tri = jnp.concatenate([tri, tri], axis=1)

    x2d = x.reshape(batch * seq, d)
    front_params = (row(g_mix[i]), bf(w_in[i]), row(mu_shift[i]), row(w0[i]),
                    bf(_pad_rows(w_decay_up[i], 0, LANES)), row(a0[i]),
                    bf(_pad_rows(w_aaa_up[i], DECAY_LORA, LANES)), bf(w_gate_up[i]),
                    row(k_k[i]), row(k_a[i]), row(r_k[i]), hsum, tri)
    (rt, at, bt, kt, v, btail, ktail, pc, bonus, rgate, zp, zg) = _front(x2d, batch, seq,
                                                                         front_params)
    recur_params = (row(ln_x_w[i]), row(ln_x_b[i]), hsum, bf(pool_w[i]), row(pool_scale[i]))
    ya, yb, (wo_b, wff1_b, wff2_b, wpg_b) = _recur(
        rt, at, bt, kt, v, btail, ktail, pc, bonus, rgate, zp, batch, seq, recur_params,
        (w_o[i], w_ff1[i], w_ff2[i], w_ple_gate[i]))
    tail_params = (row(b_gates[i]), bf(w_out_a[i]), bf(w_out_b[i]), wo_b, row(g_mlp[i]),
                   wff1_b, wff2_b, row(g_ple[i]), wpg_b, bf(w_ple_proj[i]), row(g_final))
    out = _tail(x2d, ya, yb, zg, p[i].reshape(batch * seq, PLE_DIM), tail_params)
    return out.reshape(batch, seq, d)
```

```python
import functools
import math

import jax
import jax.numpy as jnp
from jax import lax
from jax.experimental import pallas as pl
from jax.experimental.pallas import tpu as pltpu

F32 = jnp.float32
BF16 = jnp.bfloat16

D_MODEL = 1024
PLE_DIM = 256
HEAD_DIM = 64
RWKV_WIDTH = 512
HEAD_SHIFT = int(math.log2(HEAD_DIM))
DECAY_LORA = 64
AAA_LORA = 64
GATE_LORA = 128
POOL_WINDOWS = (2, 4, 8, 16)
POOL_WIDTH = 512
POOL_GROUP_DIM = POOL_WIDTH // len(POOL_WINDOWS)
MAX_WINDOW = max(POOL_WINDOWS)
RMS_EPS = 1e-6
GN_EPS = 64e-5
L2_EPS = 1e-12
RWKV_COLS = 3 * RWKV_WIDTH + DECAY_LORA + AAA_LORA + GATE_LORA
GATE_COLS = 2 * D_MODEL
D_IN = RWKV_COLS + POOL_WIDTH + GATE_COLS
LORA_OFF = 3 * RWKV_WIDTH
GATE_LORA_OFF = LORA_OFF + DECAY_LORA + AAA_LORA
GATE_OFF = RWKV_COLS + POOL_WIDTH

LANES = 128
SUBLANES = 8
HEADS_PER_BLOCK = LANES // HEAD_DIM
N_PAIRS = RWKV_WIDTH // LANES
HSUM_WIDTH = 256
CHUNK = 64
CHUNKS_PER_STEP = 8
RECUR_ROWS = CHUNK * CHUNKS_PER_STEP
N_DOUBLINGS = int(math.log2(CHUNK)) - 1
LOG_DECAY_SCALE = -math.exp(-0.5)

FRONT_ROWS = 512
PROJ_PIECE = 512
TAIL_ROWS = 512
N_TAIL_CASTS = 4
VMEM_LIMIT_BYTES = 56 * 1024 * 1024


def _head_of(channel):
    return lax.shift_right_logical(channel, HEAD_SHIFT)


def _mm(a, b):
    return jnp.dot(a.astype(BF16), b.astype(BF16), preferred_element_type=F32)


def _mm_nt(a, b):
    return lax.dot_general(a.astype(BF16), b.astype(BF16), (((1,), (1,)), ((), ())),
                           preferred_element_type=F32)


def _mm_tn(a, b):
    return lax.dot_general(a.astype(BF16), b.astype(BF16), (((0,), (0,)), ((), ())),
                           preferred_element_type=F32)


def _chunk_cumsum(tri2, x):
    hi = x.astype(BF16)
    lo = (x - hi.astype(F32)).astype(BF16)
    return jnp.dot(tri2, jnp.concatenate([hi, lo], axis=0), preferred_element_type=F32)


def _head_sum(y, hsum):
    return jnp.concatenate(
        [_mm(y[:, i:i + HSUM_WIDTH], hsum) for i in range(0, RWKV_WIDTH, HSUM_WIDTH)], axis=1)


def _rmsnorm(x, g):
    return x * lax.rsqrt(jnp.mean(x * x, axis=-1, keepdims=True) + RMS_EPS) * g


def _pool_group(gi, zp_ref, pbuf, plev, t, rows):
    win = POOL_WINDOWS[gi]
    n_levels = win.bit_length() - 1
    assert win == 1 << n_levels and win <= MAX_WINDOW
    cols = slice(gi * POOL_GROUP_DIM, (gi + 1) * POOL_GROUP_DIM)
    tok = zp_ref[:, cols]
    end = MAX_WINDOW + rows
    prev = pbuf
    for level in range(1, n_levels + 1):
        shift = 1 << (level - 1)
        first = MAX_WINDOW - (win - (1 << level))
        cur = plev.at[level % 2]
        cur[first:end, cols] = prev[first:end, cols] + prev[first - shift:end - shift, cols]
        prev = cur
    acc = prev[MAX_WINDOW:end, cols]
    pos = t * rows + lax.broadcasted_iota(jnp.int32, (rows, 1), 0)
    count = jnp.minimum(pos + 1, win).astype(F32)
    return acc / count - tok


def _front_kernel(x_ref, g_ref, w_ref, mu_ref, w0_ref, wd_ref, a0_ref, wa_ref, wg_ref,
                  kk_ref, ka_ref, rk_ref, hsum_ref, tri_ref,
                  rt_ref, at_ref, bt_ref, kt_ref, v_ref, btail_ref, ktail_ref, pc_ref,
                  bonus_ref, rgate_ref, zp_ref, zg_ref, zbuf, *, rows, tiles_per_seq):
    s = pl.program_id(0)
    t_lag = jnp.maximum(s - 1, 0) % tiles_per_seq

    @pl.when(s == 0)
    def _():
        zbuf[...] = jnp.zeros(zbuf.shape, F32)

    @pl.when(t_lag == 0)
    def _():
        zbuf[0:SUBLANES, :] = jnp.zeros((SUBLANES, RWKV_COLS), F32)

    z = zbuf[SUBLANES:SUBLANES + rows, :]
    z_prev = zbuf[SUBLANES - 1:SUBLANES - 1 + rows, :]
    zs = z + (z_prev - z) * mu_ref[...]
    r = zs[:, 0:RWKV_WIDTH]
    k = zs[:, RWKV_WIDTH:2 * RWKV_WIDTH]
    v = zs[:, 2 * RWKV_WIDTH:3 * RWKV_WIDTH]
    x_lora = zs[:, LORA_OFF:LORA_OFF + LANES]
    xg = zs[:, GATE_LORA_OFF:GATE_LORA_OFF + GATE_LORA]

    h = _rmsnorm(x_ref[...], g_ref[...]).astype(BF16)
    zbuf[0:SUBLANES, :] = z[rows - SUBLANES:, :]

    def project(first, last):
        y = jnp.dot(h, w_ref[:, first:last], preferred_element_type=F32)
        if last <= RWKV_COLS:
            zbuf[SUBLANES:SUBLANES + rows, first:last] = y
        elif last <= GATE_OFF:
            zp_ref[:, first - RWKV_COLS:last - RWKV_COLS] = y
        else:
            zg_ref[:, first - GATE_OFF:last - GATE_OFF] = y

    pieces = iter([(c, min(c + PROJ_PIECE, RWKV_COLS)) for c in range(0, RWKV_COLS, PROJ_PIECE)]
                  + [(c, c + PROJ_PIECE) for c in range(RWKV_COLS, D_IN, PROJ_PIECE)])

    def project_next(n):
        for _ in range(n):
            project(*next(pieces))

    hsum = hsum_ref[...]
    project_next(1)
    dw = _mm(jnp.tanh(x_lora), wd_ref[...])
    da = _mm(x_lora, wa_ref[...])
    rgate_ref[...] = _mm(jax.nn.sigmoid(xg), wg_ref[...])
    kk = k * kk_ref[...]
    ss = _head_sum(kk * kk, hsum)
    project_next(1)
    logw = LOG_DECAY_SCALE * jax.nn.sigmoid(w0_ref[...] + dw)
    a = jax.nn.sigmoid(a0_ref[...] + da)
    k = k * (1.0 + (a - 1.0) * ka_ref[...])
    n_chunks = rows // CHUNK
    tri2 = tri_ref[...]
    cums = [_chunk_cumsum(tri2, logw[c * CHUNK:(c + 1) * CHUNK, :]) for c in range(n_chunks)]
    bonus_sum = _head_sum(r * k * rk_ref[...], hsum)
    project_next(2)

    kk = kk * jnp.minimum(lax.rsqrt(ss), 1.0 / L2_EPS)
    beta = kk * a
    cum = jnp.concatenate(cums, axis=0)
    p_rows = [jnp.exp(x[CHUNK - 1:CHUNK, :]) for x in cums]
    p_full = jnp.concatenate([jnp.broadcast_to(x, (CHUNK, RWKV_WIDTH)) for x in p_rows], axis=0)
    pc_ref[...] = jnp.concatenate(
        [jnp.broadcast_to(x, (SUBLANES, RWKV_WIDTH)) for x in p_rows], axis=0)
    e_neg = jnp.exp(-cum)
    e_tail = p_full * e_neg
    rt_ref[...] = r * jnp.exp(cum)
    at_ref[...] = (-kk * jnp.exp(cum - logw)).astype(BF16)
    bt_ref[...] = (beta * e_neg).astype(BF16)
    kt_ref[...] = (k * e_neg).astype(BF16)
    btail_ref[...] = (beta * e_tail).astype(BF16)
    ktail_ref[...] = (k * e_tail).astype(BF16)
    v_ref[...] = v.astype(BF16)
    bonus_ref[...] = bonus_sum * v
    project_next(5)
    assert next(pieces, None) is None


def _front(x2d, batch, seq, params):
    assert (D_IN - RWKV_COLS) % PROJ_PIECE == 0
    tiles_per_seq = seq // FRONT_ROWS
    n_tiles = batch * tiles_per_seq
    m = batch * seq
    const = lambda arr: pl.BlockSpec(arr.shape, lambda s: (0,) * arr.ndim,
                                     pipeline_mode=pl.Buffered(1))
    cur = lambda s: (jnp.minimum(s, n_tiles - 1), 0)
    lag = lambda s: (jnp.maximum(s - 1, 0), 0)
    lagged = lambda width: pl.BlockSpec((FRONT_ROWS, width), lag)
    pc_rows = FRONT_ROWS // CHUNK * SUBLANES
    f32 = lambda r, w: jax.ShapeDtypeStruct((r, w), F32)
    bf16 = lambda r, w: jax.ShapeDtypeStruct((r, w), BF16)
    w = RWKV_WIDTH
    return pl.pallas_call(
        functools.partial(_front_kernel, rows=FRONT_ROWS, tiles_per_seq=tiles_per_seq),
        grid=(n_tiles + 1,),
        in_specs=[pl.BlockSpec((FRONT_ROWS, D_MODEL), cur)] + [const(p) for p in params],
        out_specs=[lagged(w)] * 7 + [pl.BlockSpec((pc_rows, w), lag)]
                  + [lagged(w), lagged(w),
                     pl.BlockSpec((FRONT_ROWS, POOL_WIDTH), cur),
                     pl.BlockSpec((FRONT_ROWS, GATE_COLS), cur)],
        out_shape=[f32(m, w)] + [bf16(m, w)] * 6 + [f32(m // CHUNK * SUBLANES, w)]
                  + [f32(m, w), f32(m, w), f32(m, POOL_WIDTH), f32(m, GATE_COLS)],
        scratch_shapes=[pltpu.VMEM((FRONT_ROWS + SUBLANES, RWKV_COLS), F32)],
        compiler_params=pltpu.CompilerParams(dimension_semantics=("arbitrary",),
                                             vmem_limit_bytes=VMEM_LIMIT_BYTES),
        name="front",
    )(x2d, *params)


def _recur_kernel(rt_ref, at_ref, bt_ref, kt_ref, v_ref, btail_ref, ktail_ref, pc_ref, bonus_ref,
                  rgate_ref, zp_ref, lnw_ref, lnb_ref, hsum_ref, poolw_ref, pscale_ref, *rest):
    n_cast = N_TAIL_CASTS
    cast_in, (out_ref, yb_ref) = rest[:n_cast], rest[n_cast:n_cast + 2]
    cast_out, (g_state, pbuf, plev) = rest[n_cast + 2:2 * n_cast + 2], rest[2 * n_cast + 2:]
    t = pl.program_id(1)

    for src, dst in zip(cast_in, cast_out):
        dst[...] = src[...].astype(BF16)

    @pl.when(t == 0)
    def _():
        g_state[...] = jnp.zeros(g_state.shape, F32)
        pbuf[0:MAX_WINDOW, :] = jnp.zeros((MAX_WINDOW, POOL_WIDTH), F32)

    pbuf[MAX_WINDOW:MAX_WINDOW + RECUR_ROWS, :] = zp_ref[...]
    pooled = []

    row = lax.broadcasted_iota(jnp.int32, (CHUNK, LANES), 0)
    col = lax.broadcasted_iota(jnp.int32, (CHUNK, LANES), 1) & (HEAD_DIM - 1)
    strict_lower = col < row
    lower = col <= row
    eye = (col == row).astype(F32)
    lane = lax.broadcasted_iota(jnp.int32, (1, LANES), 1)
    head_masks = [(_head_of(lane) == h).astype(BF16) for h in range(HEADS_PER_BLOCK)]
    head_masks2 = [jnp.concatenate([m, m], axis=1) for m in head_masks]
    brow = lax.broadcasted_iota(jnp.int32, (LANES, LANES), 0)
    bcol = lax.broadcasted_iota(jnp.int32, (LANES, LANES), 1)
    same_head = _head_of(brow) == _head_of(bcol)
    diag = brow == bcol

    def block_diag(x):
        xb = x.astype(BF16)
        masks = head_masks if x.shape[1] == LANES else head_masks2
        return jnp.concatenate([xb * m for m in masks], axis=0)

    stack = lambda x, y: jnp.concatenate([x.astype(BF16), y.astype(BF16)], axis=0)
    no_w2 = jnp.zeros((CHUNK, LANES), BF16)

    def block_levels(chunks, out, with_pooling):
        blocks = [(slice(c * CHUNK, (c + 1) * CHUNK), slice(p * LANES, (p + 1) * LANES))
                  for c in chunks for p in range(N_PAIRS)]
        a_ps = [at_ref[blk] for blk in blocks]
        r_ps = [rt_ref[blk] for blk in blocks]
        v_ps = [v_ref[blk] for blk in blocks]
        v_bd = [block_diag(x) for x in v_ps]
        tri4 = [_mm_nt(stack(x, y),
                       jnp.concatenate([block_diag(bt_ref[blk]), block_diag(kt_ref[blk])], axis=0))
                for x, y, blk in zip(a_ps, r_ps, blocks)]
        a_ab = [jnp.where(strict_lower, x[:CHUNK, :LANES], 0.0) for x in tri4]
        a_ak = [jnp.where(strict_lower, x[:CHUNK, LANES:], 0.0) for x in tri4]
        a_rb = [jnp.where(lower, x[CHUNK:, :LANES], 0.0) for x in tri4]
        a_rk = [jnp.where(lower, x[CHUNK:, LANES:], 0.0) for x in tri4]
        yield
        xk = [_mm(stack(x, y), z) for x, y, z in zip(a_ak, a_rk, v_bd)]
        x1 = [x[:CHUNK] for x in xk]
        yk = [x[CHUNK:] for x in xk]
        t_inv = [eye + x for x in a_ab]
        pw = [_mm(x, block_diag(x)) for x in a_ab]
        yield
        for level in range(N_DOUBLINGS - 1):
            both = [_mm(stack(x, y), block_diag(x)) for x, y in zip(pw, t_inv)]
            pw = [x[:CHUNK] for x in both]
            t_inv = [x + y[CHUNK:] for x, y in zip(t_inv, both)]
            if with_pooling and level < len(POOL_WINDOWS):
                pooled.append(_mm(_pool_group(level, zp_ref, pbuf, plev, t, RECUR_ROWS),
                                  poolw_ref[level]))
            yield
        t_inv = [x + _mm(x, block_diag(y)) for x, y in zip(t_inv, pw)]
        yield
        w12 = [_mm(x, block_diag(jnp.concatenate([y.astype(BF16), z], axis=1)))
               for x, y, z in zip(t_inv, x1, a_ps)]
        yield
        p_last = [pc_ref[c * SUBLANES:c * SUBLANES + 1, p * LANES:(p + 1) * LANES]
                  for c in chunks for p in range(N_PAIRS)]
        mn = [_mm_tn(stack(btail_ref[blk], ktail_ref[blk]),
                     jnp.concatenate([jnp.concatenate([w12[q][:, LANES:], w12[q][:, :LANES]], axis=1)
                                      .astype(BF16), jnp.concatenate([no_w2, v_ps[q]], axis=1)], axis=0))
              for q, blk in enumerate(blocks)]
        out["m_t"] = [jnp.where(same_head, x[:, :LANES], 0.0) + jnp.where(diag, p_last[q], 0.0)
                      for q, x in enumerate(mn)]
        out["n_t"] = [jnp.where(same_head, x[:, LANES:], 0.0) for x in mn]
        out.update(a_rb=a_rb, w12=w12, yk=yk, r_ps=r_ps)
        yield

    assert N_DOUBLINGS - 1 >= len(POOL_WINDOWS)
    half = CHUNKS_PER_STEP // 2
    groups = [range(0, half), range(half, CHUNKS_PER_STEP)]
    res = [{}, {}]
    states = [[g_state[p] for p in range(N_PAIRS)]]
    yr = []

    def chain_step(c):
        g = res[c // half]
        qs = range((c % half) * N_PAIRS, (c % half + 1) * N_PAIRS)
        states.append([_mm(g["m_t"][q], states[c][p]) + g["n_t"][q] for p, q in enumerate(qs)])
        yr.extend(_mm(g["a_rb"][q], block_diag(g["w12"][q])) for q in qs)

    for _ in block_levels(groups[0], res[0], True):
        pass
    pbuf[0:MAX_WINDOW, :] = zp_ref[RECUR_ROWS - MAX_WINDOW:, :]
    pending = list(groups[0])
    for _ in block_levels(groups[1], res[1], False):
        if pending:
            chain_step(pending.pop(0))
    for c in pending + list(groups[1]):
        chain_step(c)
    for p in range(N_PAIRS):
        g_state[p] = states[CHUNKS_PER_STEP][p]
    yk = res[0]["yk"] + res[1]["yk"]
    r_ps = res[0]["r_ps"] + res[1]["r_ps"]
    ys = [yk[q] + yr[q][:, :LANES] + _mm(r_ps[q] + yr[q][:, LANES:], states[q // N_PAIRS][q % N_PAIRS])
          for q in range(CHUNKS_PER_STEP * N_PAIRS)]
    y = jnp.concatenate([jnp.concatenate(ys[c * N_PAIRS:(c + 1) * N_PAIRS], axis=1)
                         for c in range(CHUNKS_PER_STEP)], axis=0)

    yb_ref[...] = jnp.concatenate(pooled, axis=1) * pscale_ref[...]

    hsum = hsum_ref[...]
    mean = _head_sum(y, hsum) * (1.0 / HEAD_DIM)
    d = y - mean
    var = _head_sum(d * d, hsum) * (1.0 / HEAD_DIM)
    yn = d * lax.rsqrt(var + GN_EPS) * lnw_ref[...] + lnb_ref[...]
    out_ref[...] = (yn + bonus_ref[...]) * rgate_ref[...]


def _recur(rt, at, bt, kt, v, btail, ktail, pc, bonus, rgate, zp, batch, seq, params, tail_weights):
    assert RWKV_WIDTH == POOL_WIDTH
    assert len(tail_weights) == N_TAIL_CASTS
    n_t = seq // RECUR_ROWS
    n_steps = batch * n_t
    step = lambda b, t: (b * n_t + t, 0)
    const = lambda arr: pl.BlockSpec(arr.shape, lambda b, t: (0,) * arr.ndim,
                                     pipeline_mode=pl.Buffered(1))
    rows = pl.BlockSpec((RECUR_ROWS, RWKV_WIDTH), step)
    pc_rows = pl.BlockSpec((CHUNKS_PER_STEP * SUBLANES, RWKV_WIDTH), step)
    out = jax.ShapeDtypeStruct((batch * seq, RWKV_WIDTH), F32)
    for w in tail_weights:
        assert w.shape[0] % (n_steps * 2 * SUBLANES) == 0, w.shape
    w_slices = [pl.BlockSpec((w.shape[0] // n_steps, w.shape[1]), step) for w in tail_weights]
    results = pl.pallas_call(
        _recur_kernel,
        grid=(batch, n_t),
        in_specs=[rows] * 7 + [pc_rows, rows, rows, rows] + [const(p) for p in params] + w_slices,
        out_specs=[rows, rows] + w_slices,
        out_shape=[out, out] + [jax.ShapeDtypeStruct(w.shape, BF16) for w in tail_weights],
        scratch_shapes=[pltpu.VMEM((N_PAIRS, LANES, LANES), F32),
                        pltpu.VMEM((RECUR_ROWS + MAX_WINDOW, POOL_WIDTH), F32),
                        pltpu.VMEM((2, RECUR_ROWS + MAX_WINDOW, POOL_WIDTH), F32)],
        compiler_params=pltpu.CompilerParams(dimension_semantics=("arbitrary", "arbitrary"),
                                             vmem_limit_bytes=VMEM_LIMIT_BYTES),
        name="recur",
    )(rt, at, bt, kt, v, btail, ktail, pc, bonus, rgate, zp, *params, *tail_weights)
    return results[0], results[1], results[2:]


def _tail_kernel(x_ref, ya_ref, yb_ref, zg_ref, p_ref, bg_ref, woa_ref, wob_ref,
                 wo_ref, gmlp_ref, w1_ref, w2_ref, gple_ref, wpg_ref, wpe_ref, gfin_ref, out_ref):
    y_a = _mm(ya_ref[...], woa_ref[...])
    y_b = _mm(yb_ref[...], wob_ref[...])
    ple = _mm(p_ref[...], wpe_ref[...])
    gate = jax.nn.sigmoid(zg_ref[...] + bg_ref[...])
    merged = gate[:, :D_MODEL] * y_a + gate[:, D_MODEL:] * y_b
    x = x_ref[...] + _mm(merged, wo_ref[...])

    f = jnp.maximum(_mm(_rmsnorm(x, gmlp_ref[...]), w1_ref[...]), 0.0)
    x = x + _mm(f * f, w2_ref[...])

    n = x.shape[0]
    halves = (slice(0, n // 2), slice(n // 2, n))
    gate_pre = [_mm(_rmsnorm(x[r], gple_ref[...]), wpg_ref[...]) for r in halves]
    for r, pre in zip(halves, gate_pre):
        out_ref[r, :] = _rmsnorm(x[r] + jax.nn.sigmoid(pre) * ple[r], gfin_ref[...])


def _tail(x2d, ya, yb, zg, p2d, params):
    m = x2d.shape[0]
    rows = lambda width: pl.BlockSpec((TAIL_ROWS, width), lambda i: (i, 0))
    const = lambda arr: pl.BlockSpec(arr.shape, lambda i: (0,) * arr.ndim,
                                     pipeline_mode=pl.Buffered(1))
    return pl.pallas_call(
        _tail_kernel,
        grid=(m // TAIL_ROWS,),
        in_specs=[rows(D_MODEL), rows(RWKV_WIDTH), rows(POOL_WIDTH), rows(GATE_COLS), rows(PLE_DIM)]
                 + [const(p) for p in params],
        out_specs=rows(D_MODEL),
        out_shape=jax.ShapeDtypeStruct((m, D_MODEL), F32),
        compiler_params=pltpu.CompilerParams(dimension_semantics=("arbitrary",),
                                             vmem_limit_bytes=VMEM_LIMIT_BYTES),
        name="tail",
    )(x2d, ya, yb, zg, p2d, *params)


def _pad_rows(w, before, total):
    return jnp.pad(w, ((before, total - before - w.shape[0]), (0, 0)))


def kernel(x, p, g_mix, w_in, mu_shift, w0, w_decay_up, a0, w_aaa_up, w_gate_up, k_k, k_a, r_k, ln_x_w, ln_x_b, pool_w, pool_scale, b_gates, w_out_a, w_out_b, w_o, g_mlp, w_ff1, w_ff2, g_ple, w_ple_gate, w_ple_proj, g_final):
    batch, seq, d = x.shape
    depth = w_in.shape[0]
    assert d == D_MODEL and seq % FRONT_ROWS == 0 and seq % RECUR_ROWS == 0
    assert (batch * seq) % TAIL_ROWS == 0
    assert depth == 1, "the fused tail kernel supports a single layer"
    i = 0
    row = lambda a: a.reshape(1, -1)
    bf = lambda a: a.astype(BF16)

    head_id = jnp.arange(HSUM_WIDTH) // HEAD_DIM
    hsum = (head_id[:, None] == head_id[None, :]).astype(BF16)
    tok = jnp.arange(CHUNK)
    tri = (tok[None, :] <= tok[:, None]).astype(BF16)
    tri = jnp.concatenate([tri, tri], axis=1)

    x2d = x.reshape(batch * seq, d)
    front_params = (row(g_mix[i]), bf(w_in[i]), row(mu_shift[i]), row(w0[i]),
                    bf(_pad_rows(w_decay_up[i], 0, LANES)), row(a0[i]),
                    bf(_pad_rows(w_aaa_up[i], DECAY_LORA, LANES)), bf(w_gate_up[i]),
                    row(k_k[i]), row(k_a[i]), row(r_k[i]), hsum, tri)
    (rt, at, bt, kt, v, btail, ktail, pc, bonus, rgate, zp, zg) = _front(x2d, batch, seq,
                                                                         front_params)
    recur_params = (row(ln_x_w[i]), row(ln_x_b[i]), hsum, bf(pool_w[i]), row(pool_scale[i]))
    ya, yb, (wo_b, wff1_b, wff2_b, wpg_b) = _recur(
        rt, at, bt, kt, v, btail, ktail, pc, bonus, rgate, zp, batch, seq, recur_params,
        (w_o[i], w_ff1[i], w_ff2[i], w_ple_gate[i]))
    tail_params = (row(b_gates[i]), bf(w_out_a[i]), bf(w_out_b[i]), wo_b, row(g_mlp[i]),
                   wff1_b, wff2_b, row(g_ple[i]), wpg_b, bf(w_ple_proj[i]), row(g_final))
    out = _tail(x2d, ya, yb, zg, p[i].reshape(batch * seq, PLE_DIM), tail_params)
    return out.reshape(batch, seq, d)
```

```python
import functools
import math

import jax
import jax.numpy as jnp
from jax import lax
from jax.experimental import pallas as pl
from jax.experimental.pallas import tpu as pltpu

F32 = jnp.float32
BF16 = jnp.bfloat16

D_MODEL = 1024
PLE_DIM = 256
HEAD_DIM = 64
RWKV_WIDTH = 512
HEAD_SHIFT = int(math.log2(HEAD_DIM))
DECAY_LORA = 64
AAA_LORA = 64
GATE_LORA = 128
POOL_WINDOWS = (2, 4, 8, 16)
POOL_WIDTH = 512
POOL_GROUP_DIM = POOL_WIDTH // len(POOL_WINDOWS)
MAX_WINDOW = max(POOL_WINDOWS)
RMS_EPS = 1e-6
GN_EPS = 64e-5
L2_EPS = 1e-12
RWKV_COLS = 3 * RWKV_WIDTH + DECAY_LORA + AAA_LORA + GATE_LORA
GATE_COLS = 2 * D_MODEL
D_IN = RWKV_COLS + POOL_WIDTH + GATE_COLS
LORA_OFF = 3 * RWKV_WIDTH
GATE_LORA_OFF = LORA_OFF + DECAY_LORA + AAA_LORA
GATE_OFF = RWKV_COLS + POOL_WIDTH

LANES = 128
SUBLANES = 8
HEADS_PER_BLOCK = LANES // HEAD_DIM
N_PAIRS = RWKV_WIDTH // LANES
HSUM_WIDTH = 256
CHUNK = 64
CHUNKS_PER_STEP = 8
RECUR_ROWS = CHUNK * CHUNKS_PER_STEP
N_DOUBLINGS = int(math.log2(CHUNK)) - 1
LOG_DECAY_SCALE = -math.exp(-0.5)

FRONT_ROWS = 512
PROJ_PIECE = 512
TAIL_ROWS = 512
N_TAIL_CASTS = 4
VMEM_LIMIT_BYTES = 56 * 1024 * 1024


def _head_of(channel):
    return lax.shift_right_logical(channel, HEAD_SHIFT)


def _mm(a, b):
    return jnp.dot(a.astype(BF16), b.astype(BF16), preferred_element_type=F32)


def _mm_nt(a, b):
    return lax.dot_general(a.astype(BF16), b.astype(BF16), (((1,), (1,)), ((), ())),
                           preferred_element_type=F32)


def _mm_tn(a, b):
    return lax.dot_general(a.astype(BF16), b.astype(BF16), (((0,), (0,)), ((), ())),
                           preferred_element_type=F32)


def _chunk_cumsum(tri2, x):
    hi = x.astype(BF16)
    lo = (x - hi.astype(F32)).astype(BF16)
    return jnp.dot(tri2, jnp.concatenate([hi, lo], axis=0), preferred_element_type=F32)


def _head_sum(y, hsum):
    return jnp.concatenate(
        [_mm(y[:, i:i + HSUM_WIDTH], hsum) for i in range(0, RWKV_WIDTH, HSUM_WIDTH)], axis=1)


def _rmsnorm(x, g):
    return x * lax.rsqrt(jnp.mean(x * x, axis=-1, keepdims=True) + RMS_EPS) * g


def _pool_group(gi, zp_ref, pbuf, plev, t, rows):
    win = POOL_WINDOWS[gi]
    n_levels = win.bit_length() - 1
    assert win == 1 << n_levels and win <= MAX_WINDOW
    cols = slice(gi * POOL_GROUP_DIM, (gi + 1) * POOL_GROUP_DIM)
    tok = zp_ref[:, cols]
    end = MAX_WINDOW + rows
    prev = pbuf
    for level in range(1, n_levels + 1):
        shift = 1 << (level - 1)
        first = MAX_WINDOW - (win - (1 << level))
        cur = plev.at[level % 2]
        cur[first:end, cols] = prev[first:end, cols] + prev[first - shift:end - shift, cols]
        prev = cur
    acc = prev[MAX_WINDOW:end, cols]
    pos = t * rows + lax.broadcasted_iota(jnp.int32, (rows, 1), 0)
    count = jnp.minimum(pos + 1, win).astype(F32)
    return acc / count - tok


def _front_kernel(x_ref, g_ref, w_ref, mu_ref, w0_ref, wd_ref, a0_ref, wa_ref, wg_ref,
                  kk_ref, ka_ref, rk_ref, hsum_ref, tri_ref,
                  rt_ref, at_ref, bt_ref, kt_ref, v_ref, btail_ref, ktail_ref, pc_ref,
                  bonus_ref, rgate_ref, zp_ref, zg_ref, zbuf, *, rows, tiles_per_seq):
    s = pl.program_id(0)
    t_lag = jnp.maximum(s - 1, 0) % tiles_per_seq

    @pl.when(s == 0)
    def _():
        zbuf[...] = jnp.zeros(zbuf.shape, F32)

    @pl.when(t_lag == 0)
    def _():
        zbuf[0:SUBLANES, :] = jnp.zeros((SUBLANES, RWKV_COLS), F32)

    z = zbuf[SUBLANES:SUBLANES + rows, :]
    z_prev = zbuf[SUBLANES - 1:SUBLANES - 1 + rows, :]
    zs = z + (z_prev - z) * mu_ref[...]
    r = zs[:, 0:RWKV_WIDTH]
    k = zs[:, RWKV_WIDTH:2 * RWKV_WIDTH]
    v = zs[:, 2 * RWKV_WIDTH:3 * RWKV_WIDTH]
    x_lora = zs[:, LORA_OFF:LORA_OFF + LANES]
    xg = zs[:, GATE_LORA_OFF:GATE_LORA_OFF + GATE_LORA]

    h = _rmsnorm(x_ref[...], g_ref[...]).astype(BF16)
    zbuf[0:SUBLANES, :] = z[rows - SUBLANES:, :]

    def project(first, last):
        y = jnp.dot(h, w_ref[:, first:last], preferred_element_type=F32)
        if last <= RWKV_COLS:
            zbuf[SUBLANES:SUBLANES + rows, first:last] = y
        elif last <= GATE_OFF:
            zp_ref[:, first - RWKV_COLS:last - RWKV_COLS] = y
        else:
            zg_ref[:, first - GATE_OFF:last - GATE_OFF] = y

    pieces = iter([(c, min(c + PROJ_PIECE, RWKV_COLS)) for c in range(0, RWKV_COLS, PROJ_PIECE)]
                  + [(c, c + PROJ_PIECE) for c in range(RWKV_COLS, D_IN, PROJ_PIECE)])

    def project_next(n):
        for _ in range(n):
            project(*next(pieces))

    hsum = hsum_ref[...]
    project_next(1)
    dw = _mm(jnp.tanh(x_lora), wd_ref[...])
    da = _mm(x_lora, wa_ref[...])
    rgate_ref[...] = _mm(jax.nn.sigmoid(xg), wg_ref[...])
    kk = k * kk_ref[...]
    ss = _head_sum(kk * kk, hsum)
    project_next(1)
    logw = LOG_DECAY_SCALE * jax.nn.sigmoid(w0_ref[...] + dw)
    a = jax.nn.sigmoid(a0_ref[...] + da)
    k = k * (1.0 + (a - 1.0) * ka_ref[...])
    n_chunks = rows // CHUNK
    tri2 = tri_ref[...]
    cums = [_chunk_cumsum(tri2, logw[c * CHUNK:(c + 1) * CHUNK, :]) for c in range(n_chunks)]
    bonus_sum = _head_sum(r * k * rk_ref[...], hsum)
    project_next(2)

    kk = kk * jnp.minimum(lax.rsqrt(ss), 1.0 / L2_EPS)
    beta = kk * a
    cum = jnp.concatenate(cums, axis=0)
    p_rows = [jnp.exp(x[CHUNK - 1:CHUNK, :]) for x in cums]
    p_full = jnp.concatenate([jnp.broadcast_to(x, (CHUNK, RWKV_WIDTH)) for x in p_rows], axis=0)
    pc_ref[...] = jnp.concatenate(
        [jnp.broadcast_to(x, (SUBLANES, RWKV_WIDTH)) for x in p_rows], axis=0)
    e_neg = jnp.exp(-cum)
    e_tail = p_full * e_neg
    rt_ref[...] = r * jnp.exp(cum)
    at_ref[...] = (-kk * jnp.exp(cum - logw)).astype(BF16)
    bt_ref[...] = (beta * e_neg).astype(BF16)
    kt_ref[...] = (k * e_neg).astype(BF16)
    btail_ref[...] = (beta * e_tail).astype(BF16)
    ktail_ref[...] = (k * e_tail).astype(BF16)
    v_ref[...] = v.astype(BF16)
    bonus_ref[...] = bonus_sum * v
    project_next(5)
    assert next(pieces, None) is None


def _front(x2d, batch, seq, params):
    assert (D_IN - RWKV_COLS) % PROJ_PIECE == 0
    tiles_per_seq = seq // FRONT_ROWS
    n_tiles = batch * tiles_per_seq
    m = batch * seq
    const = lambda arr: pl.BlockSpec(arr.shape, lambda s: (0,) * arr.ndim,
                                     pipeline_mode=pl.Buffered(1))
    cur = lambda s: (jnp.minimum(s, n_tiles - 1), 0)
    lag = lambda s: (jnp.maximum(s - 1, 0), 0)
    lagged = lambda width: pl.BlockSpec((FRONT_ROWS, width), lag)
    pc_rows = FRONT_ROWS // CHUNK * SUBLANES
    f32 = lambda r, w: jax.ShapeDtypeStruct((r, w), F32)
    bf16 = lambda r, w: jax.ShapeDtypeStruct((r, w), BF16)
    w = RWKV_WIDTH
    return pl.pallas_call(
        functools.partial(_front_kernel, rows=FRONT_ROWS, tiles_per_seq=tiles_per_seq),
        grid=(n_tiles + 1,),
        in_specs=[pl.BlockSpec((FRONT_ROWS, D_MODEL), cur)] + [const(p) for p in params],
        out_specs=[lagged(w)] * 7 + [pl.BlockSpec((pc_rows, w), lag)]
                  + [lagged(w), lagged(w),
                     pl.BlockSpec((FRONT_ROWS, POOL_WIDTH), cur),
                     pl.BlockSpec((FRONT_ROWS, GATE_COLS), cur)],
        out_shape=[f32(m, w)] + [bf16(m, w)] * 6 + [f32(m // CHUNK * SUBLANES, w)]
                  + [f32(m, w), f32(m, w), f32(m, POOL_WIDTH), f32(m, GATE_COLS)],
        scratch_shapes=[pltpu.VMEM((FRONT_ROWS + SUBLANES, RWKV_COLS), F32)],
        compiler_params=pltpu.CompilerParams(dimension_semantics=("arbitrary",),
                                             vmem_limit_bytes=VMEM_LIMIT_BYTES),
        name="front",
    )(x2d, *params)


def _recur_kernel(rt_ref, at_ref, bt_ref, kt_ref, v_ref, btail_ref, ktail_ref, pc_ref, bonus_ref,
                  rgate_ref, zp_ref, lnw_ref, lnb_ref, hsum_ref, poolw_ref, pscale_ref, *rest):
    n_cast = N_TAIL_CASTS
    cast_in, (out_ref, yb_ref) = rest[:n_cast], rest[n_cast:n_cast + 2]
    cast_out, (g_state, pbuf, plev) = rest[n_cast + 2:2 * n_cast + 2], rest[2 * n_cast + 2:]
    t = pl.program_id(1)

    for src, dst in zip(cast_in, cast_out):
        dst[...] = src[...].astype(BF16)

    @pl.when(t == 0)
    def _():
        g_state[...] = jnp.zeros(g_state.shape, F32)
        pbuf[0:MAX_WINDOW, :] = jnp.zeros((MAX_WINDOW, POOL_WIDTH), F32)

    pbuf[MAX_WINDOW:MAX_WINDOW + RECUR_ROWS, :] = zp_ref[...]
    pooled = []

    row = lax.broadcasted_iota(jnp.int32, (CHUNK, LANES), 0)
    col = lax.broadcasted_iota(jnp.int32, (CHUNK, LANES), 1) & (HEAD_DIM - 1)
    strict_lower = col < row
    lower = col <= row
    eye = (col == row).astype(F32)
    lane = lax.broadcasted_iota(jnp.int32, (1, LANES), 1)
    head_masks = [(_head_of(lane) == h).astype(BF16) for h in range(HEADS_PER_BLOCK)]
    head_masks2 = [jnp.concatenate([m, m], axis=1) for m in head_masks]
    brow = lax.broadcasted_iota(jnp.int32, (LANES, LANES), 0)
    bcol = lax.broadcasted_iota(jnp.int32, (LANES, LANES), 1)
    same_head = _head_of(brow) == _head_of(bcol)
    diag = brow == bcol

    def block_diag(x):
        xb = x.astype(BF16)
        masks = head_masks if x.shape[1] == LANES else head_masks2
        return jnp.concatenate([xb * m for m in masks], axis=0)

    stack = lambda x, y: jnp.concatenate([x.astype(BF16), y.astype(BF16)], axis=0)
    no_w2 = jnp.zeros((CHUNK, LANES), BF16)

    def block_levels(chunks, out, with_pooling):
        blocks = [(slice(c * CHUNK, (c + 1) * CHUNK), slice(p * LANES, (p + 1) * LANES))
                  for c in chunks for p in range(N_PAIRS)]
        a_ps = [at_ref[blk] for blk in blocks]
        r_ps = [rt_ref[blk] for blk in blocks]
        v_ps = [v_ref[blk] for blk in blocks]
        v_bd = [block_diag(x) for x in v_ps]
        tri4 = [_mm_nt(stack(x, y),
                       jnp.concatenate([block_diag(bt_ref[blk]), block_diag(kt_ref[blk])], axis=0))
                for x, y, blk in zip(a_ps, r_ps, blocks)]
        a_ab = [jnp.where(strict_lower, x[:CHUNK, :LANES], 0.0) for x in tri4]
        a_ak = [jnp.where(strict_lower, x[:CHUNK, LANES:], 0.0) for x in tri4]
        a_rb = [jnp.where(lower, x[CHUNK:, :LANES], 0.0) for x in tri4]
        a_rk = [jnp.where(lower, x[CHUNK:, LANES:], 0.0) for x in tri4]
        yield
        xk = [_mm(stack(x, y), z) for x, y, z in zip(a_ak, a_rk, v_bd)]
        x1 = [x[:CHUNK] for x in xk]
        yk = [x[CHUNK:] for x in xk]
        t_inv = [eye + x for x in a_ab]
        pw = [_mm(x, block_diag(x)) for x in a_ab]
        yield
        for level in range(N_DOUBLINGS - 1):
            both = [_mm(stack(x, y), block_diag(x)) for x, y in zip(pw, t_inv)]
            pw = [x[:CHUNK] for x in both]
            t_inv = [x + y[CHUNK:] for x, y in zip(t_inv, both)]
            if with_pooling and level < len(POOL_WINDOWS):
                pooled.append(_mm(_pool_group(level, zp_ref, pbuf, plev, t, RECUR_ROWS),
                                  poolw_ref[level]))
            yield
        t_inv = [x + _mm(x, block_diag(y)) for x, y in zip(t_inv, pw)]
        yield
        w12 = [_mm(x, block_diag(jnp.concatenate([y.astype(BF16), z], axis=1)))
               for x, y, z in zip(t_inv, x1, a_ps)]
        yield
        p_last = [pc_ref[c * SUBLANES:c * SUBLANES + 1, p * LANES:(p + 1) * LANES]
                  for c in chunks for p in range(N_PAIRS)]
        mn = [_mm_tn(stack(btail_ref[blk], ktail_ref[blk]),
                     jnp.concatenate([jnp.concatenate([w12[q][:, LANES:], w12[q][:, :LANES]], axis=1)
                                      .astype(BF16), jnp.concatenate([no_w2, v_ps[q]], axis=1)], axis=0))
              for q, blk in enumerate(blocks)]
        out["m_t"] = [jnp.where(same_head, x[:, :LANES], 0.0) + jnp.where(diag, p_last[q], 0.0)
                      for q, x in enumerate(mn)]
        out["n_t"] = [jnp.where(same_head, x[:, LANES:], 0.0) for x in mn]
        yr = [_mm(x, block_diag(y)) for x, y in zip(a_rb, w12)]
        out["y1"] = [x + y[:, :LANES] for x, y in zip(yk, yr)]
        out["r2"] = [x + y[:, LANES:] for x, y in zip(r_ps, yr)]
        yield

    assert N_DOUBLINGS - 1 >= len(POOL_WINDOWS)
    half = CHUNKS_PER_STEP // 2
    groups = [range(0, half), range(half, CHUNKS_PER_STEP)]
    res = [{}, {}]
    states = [[g_state[p] for p in range(N_PAIRS)]]
    ys = []

    def chain_step(c):
        g = res[c // half]
        qs = range((c % half) * N_PAIRS, (c % half + 1) * N_PAIRS)
        both = [_mm(stack(g["m_t"][q], g["r2"][q]), states[c][p]) for p, q in enumerate(qs)]
        states.append([x[:LANES] + g["n_t"][q] for x, q in zip(both, qs)])
        ys.extend(g["y1"][q] + x[LANES:] for x, q in zip(both, qs))

    for _ in block_levels(groups[0], res[0], True):
        pass
    pbuf[0:MAX_WINDOW, :] = zp_ref[RECUR_ROWS - MAX_WINDOW:, :]
    pending = list(groups[0])
    for _ in block_levels(groups[1], res[1], False):
        if pending:
            chain_step(pending.pop(0))
    for c in pending + list(groups[1]):
        chain_step(c)
    for p in range(N_PAIRS):
        g_state[p] = states[CHUNKS_PER_STEP][p]
    y = jnp.concatenate([jnp.concatenate(ys[c * N_PAIRS:(c + 1) * N_PAIRS], axis=1)
                         for c in range(CHUNKS_PER_STEP)], axis=0)

    yb_ref[...] = jnp.concatenate(pooled, axis=1) * pscale_ref[...]

    hmean = hsum_ref[...]
    d = y - _head_sum(y, hmean)
    var = _head_sum(d * d, hmean)
    yn = d * lax.rsqrt(var + GN_EPS) * lnw_ref[...] + lnb_ref[...]
    out_ref[...] = (yn + bonus_ref[...]) * rgate_ref[...]


def _recur(rt, at, bt, kt, v, btail, ktail, pc, bonus, rgate, zp, batch, seq, params, tail_weights):
    assert RWKV_WIDTH == POOL_WIDTH
    assert len(tail_weights) == N_TAIL_CASTS
    n_t = seq // RECUR_ROWS
    n_steps = batch * n_t
    step = lambda b, t: (b * n_t + t, 0)
    const = lambda arr: pl.BlockSpec(arr.shape, lambda b, t: (0,) * arr.ndim,
                                     pipeline_mode=pl.Buffered(1))
    rows = pl.BlockSpec((RECUR_ROWS, RWKV_WIDTH), step)
    pc_rows = pl.BlockSpec((CHUNKS_PER_STEP * SUBLANES, RWKV_WIDTH), step)
    out = jax.ShapeDtypeStruct((batch * seq, RWKV_WIDTH), F32)
    for w in tail_weights:
        assert w.shape[0] % (n_steps * 2 * SUBLANES) == 0, w.shape
    w_slices = [pl.BlockSpec((w.shape[0] // n_steps, w.shape[1]), step) for w in tail_weights]
    results = pl.pallas_call(
        _recur_kernel,
        grid=(batch, n_t),
        in_specs=[rows] * 7 + [pc_rows, rows, rows, rows] + [const(p) for p in params] + w_slices,
        out_specs=[rows, rows] + w_slices,
        out_shape=[out, out] + [jax.ShapeDtypeStruct(w.shape, BF16) for w in tail_weights],
        scratch_shapes=[pltpu.VMEM((N_PAIRS, LANES, LANES), F32),
                        pltpu.VMEM((RECUR_ROWS + MAX_WINDOW, POOL_WIDTH), F32),
                        pltpu.VMEM((2, RECUR_ROWS + MAX_WINDOW, POOL_WIDTH), F32)],
        compiler_params=pltpu.CompilerParams(dimension_semantics=("arbitrary", "arbitrary"),
                                             vmem_limit_bytes=VMEM_LIMIT_BYTES),
        name="recur",
    )(rt, at, bt, kt, v, btail, ktail, pc, bonus, rgate, zp, *params, *tail_weights)
    return results[0], results[1], results[2:]


def _tail_kernel(x_ref, ya_ref, yb_ref, zg_ref, p_ref, bg_ref, woa_ref, wob_ref,
                 wo_ref, gmlp_ref, w1_ref, w2_ref, gple_ref, wpg_ref, wpe_ref, gfin_ref, out_ref):
    y_a = _mm(ya_ref[...], woa_ref[...])
    y_b = _mm(yb_ref[...], wob_ref[...])
    ple = _mm(p_ref[...], wpe_ref[...])
    gate = jax.nn.sigmoid(zg_ref[...] + bg_ref[...])
    merged = gate[:, :D_MODEL] * y_a + gate[:, D_MODEL:] * y_b
    x = x_ref[...] + _mm(merged, wo_ref[...])

    f = jnp.maximum(_mm(_rmsnorm(x, gmlp_ref[...]), w1_ref[...]), 0.0)
    x = x + _mm(f * f, w2_ref[...])

    n = x.shape[0]
    halves = (slice(0, n // 2), slice(n // 2, n))
    gate_pre = [_mm(_rmsnorm(x[r], gple_ref[...]), wpg_ref[...]) for r in halves]
    for r, pre in zip(halves, gate_pre):
        out_ref[r, :] = _rmsnorm(x[r] + jax.nn.sigmoid(pre) * ple[r], gfin_ref[...])


def _tail(x2d, ya, yb, zg, p2d, params):
    m = x2d.shape[0]
    rows = lambda width: pl.BlockSpec((TAIL_ROWS, width), lambda i: (i, 0))
    const = lambda arr: pl.BlockSpec(arr.shape, lambda i: (0,) * arr.ndim,
                                     pipeline_mode=pl.Buffered(1))
    return pl.pallas_call(
        _tail_kernel,
        grid=(m // TAIL_ROWS,),
        in_specs=[rows(D_MODEL), rows(RWKV_WIDTH), rows(POOL_WIDTH), rows(GATE_COLS), rows(PLE_DIM)]
                 + [const(p) for p in params],
        out_specs=rows(D_MODEL),
        out_shape=jax.ShapeDtypeStruct((m, D_MODEL), F32),
        compiler_params=pltpu.CompilerParams(dimension_semantics=("arbitrary",),
                                             vmem_limit_bytes=VMEM_LIMIT_BYTES),
        name="tail",
    )(x2d, ya, yb, zg, p2d, *params)


def _pad_rows(w, before, total):
    return jnp.pad(w, ((before, total - before - w.shape[0]), (0, 0)))


def kernel(x, p, g_mix, w_in, mu_shift, w0, w_decay_up, a0, w_aaa_up, w_gate_up, k_k, k_a, r_k, ln_x_w, ln_x_b, pool_w, pool_scale, b_gates, w_out_a, w_out_b, w_o, g_mlp, w_ff1, w_ff2, g_ple, w_ple_gate, w_ple_proj, g_final):
    batch, seq, d = x.shape
    depth = w_in.shape[0]
    assert d == D_MODEL and seq % FRONT_ROWS == 0 and seq % RECUR_ROWS == 0
    assert (batch * seq) % TAIL_ROWS == 0
    assert depth == 1, "the fused tail kernel supports a single layer"
    i = 0
    row = lambda a: a.reshape(1, -1)
    bf = lambda a: a.astype(BF16)

    head_id = jnp.arange(HSUM_WIDTH) // HEAD_DIM
    hsum = (head_id[:, None] == head_id[None, :]).astype(BF16)
    tok = jnp.arange(CHUNK)
    tri = (tok[None, :] <= tok[:, None]).astype(BF16)
    tri = jnp.concatenate([tri, tri], axis=1)

    x2d = x.reshape(batch * seq, d)
    front_params = (row(g_mix[i]), bf(w_in[i]), row(mu_shift[i]), row(w0[i]),
                    bf(_pad_rows(w_decay_up[i], 0, LANES)), row(a0[i]),
                    bf(_pad_rows(w_aaa_up[i], DECAY_LORA, LANES)), bf(w_gate_up[i]),
                    row(k_k[i]), row(k_a[i]), row(r_k[i]), hsum, tri)
    (rt, at, bt, kt, v, btail, ktail, pc, bonus, rgate, zp, zg) = _front(x2d, batch, seq,
                                                                         front_params)
    recur_params = (row(ln_x_w[i]), row(ln_x_b[i]), hsum * (1.0 / HEAD_DIM), bf(pool_w[i]),
                    row(pool_scale[i]))
    ya, yb, (wo_b, wff1_b, wff2_b, wpg_b) = _recur(
        rt, at, bt, kt, v, btail, ktail, pc, bonus, rgate, zp, batch, seq, recur_params,
        (w_o[i], w_ff1[i], w_ff2[i], w_ple_gate[i]))
    tail_params = (row(b_gates[i]), bf(w_out_a[i]), bf(w_out_b[i]), wo_b, row(g_mlp[i]),
                   wff1_b, wff2_b, row(g_ple[i]), wpg_b, bf(w_ple_proj[i]), row(g_final))
    out = _tail(x2d, ya, yb, zg, p[i].reshape(batch * seq, PLE_DIM), tail_params)
    return out.reshape(batch, seq, d)
```

```python
import functools
import math

import jax
import jax.numpy as jnp
from jax import lax
from jax.experimental import pallas as pl
from jax.experimental.pallas import tpu as pltpu

F32 = jnp.float32
BF16 = jnp.bfloat16

D_MODEL = 1024
PLE_DIM = 256
HEAD_DIM = 64
RWKV_WIDTH = 512
HEAD_SHIFT = int(math.log2(HEAD_DIM))
DECAY_LORA = 64
AAA_LORA = 64
GATE_LORA = 128
POOL_WINDOWS = (2, 4, 8, 16)
POOL_WIDTH = 512
POOL_GROUP_DIM = POOL_WIDTH // len(POOL_WINDOWS)
MAX_WINDOW = max(POOL_WINDOWS)
RMS_EPS = 1e-6
GN_EPS = 64e-5
L2_EPS = 1e-12
RWKV_COLS = 3 * RWKV_WIDTH + DECAY_LORA + AAA_LORA + GATE_LORA
GATE_COLS = 2 * D_MODEL
D_IN = RWKV_COLS + POOL_WIDTH + GATE_COLS
LORA_OFF = 3 * RWKV_WIDTH
GATE_LORA_OFF = LORA_OFF + DECAY_LORA + AAA_LORA
GATE_OFF = RWKV_COLS + POOL_WIDTH

LANES = 128
SUBLANES = 8
HEADS_PER_BLOCK = LANES // HEAD_DIM
N_PAIRS = RWKV_WIDTH // LANES
HSUM_WIDTH = 256
CHUNK = 64
CHUNKS_PER_STEP = 8
RECUR_ROWS = CHUNK * CHUNKS_PER_STEP
N_DOUBLINGS = int(math.log2(CHUNK)) - 1
LOG_DECAY_SCALE = -math.exp(-0.5)

FRONT_ROWS = 512
PROJ_PIECE = 512
TAIL_ROWS = 512
N_TAIL_CASTS = 4
VMEM_LIMIT_BYTES = 56 * 1024 * 1024


def _head_of(channel):
    return lax.shift_right_logical(channel, HEAD_SHIFT)


def _mm(a, b):
    return jnp.dot(a.astype(BF16), b.astype(BF16), preferred_element_type=F32)


def _mm_nt(a, b):
    return lax.dot_general(a.astype(BF16), b.astype(BF16), (((1,), (1,)), ((), ())),
                           preferred_element_type=F32)


def _mm_tn(a, b):
    return lax.dot_general(a.astype(BF16), b.astype(BF16), (((0,), (0,)), ((), ())),
                           preferred_element_type=F32)


def _chunk_cumsum(tri2, x):
    hi = x.astype(BF16)
    lo = (x - hi.astype(F32)).astype(BF16)
    return jnp.dot(tri2, jnp.concatenate([hi, lo], axis=0), preferred_element_type=F32)


def _head_sum(y, hsum):
    return jnp.concatenate(
        [_mm(y[:, i:i + HSUM_WIDTH], hsum) for i in range(0, RWKV_WIDTH, HSUM_WIDTH)], axis=1)


def _rmsnorm(x, g):
    return x * lax.rsqrt(jnp.mean(x * x, axis=-1, keepdims=True) + RMS_EPS) * g


def _pool_group(gi, zp_ref, pbuf, plev, t, rows):
    win = POOL_WINDOWS[gi]
    n_levels = win.bit_length() - 1
    assert win == 1 << n_levels and win <= MAX_WINDOW
    cols = slice(gi * POOL_GROUP_DIM, (gi + 1) * POOL_GROUP_DIM)
    tok = zp_ref[:, cols]
    end = MAX_WINDOW + rows
    prev = pbuf
    for level in range(1, n_levels + 1):
        shift = 1 << (level - 1)
        first = MAX_WINDOW - (win - (1 << level))
        cur = plev.at[level % 2]
        cur[first:end, cols] = prev[first:end, cols] + prev[first - shift:end - shift, cols]
        prev = cur
    acc = prev[MAX_WINDOW:end, cols]
    pos = t * rows + lax.broadcasted_iota(jnp.int32, (rows, 1), 0)
    count = jnp.minimum(pos + 1, win).astype(F32)
    return acc / count - tok


def _front_kernel(x_ref, g_ref, w_ref, mu_ref, w0_ref, wd_ref, a0_ref, wa_ref, wg_ref,
                  kk_ref, ka_ref, rk_ref, hsum_ref, tri_ref,
                  rt_ref, at_ref, bt_ref, kt_ref, v_ref, btail_ref, ktail_ref, pc_ref,
                  bonus_ref, rgate_ref, zp_ref, zg_ref, zbuf, *, rows, tiles_per_seq):
    s = pl.program_id(0)
    t_lag = jnp.maximum(s - 1, 0) % tiles_per_seq

    @pl.when(s == 0)
    def _():
        zbuf[...] = jnp.zeros(zbuf.shape, F32)

    @pl.when(t_lag == 0)
    def _():
        zbuf[0:SUBLANES, :] = jnp.zeros((SUBLANES, RWKV_COLS), F32)

    z = zbuf[SUBLANES:SUBLANES + rows, :]
    z_prev = zbuf[SUBLANES - 1:SUBLANES - 1 + rows, :]
    zs = z + (z_prev - z) * mu_ref[...]
    r = zs[:, 0:RWKV_WIDTH]
    k = zs[:, RWKV_WIDTH:2 * RWKV_WIDTH]
    v = zs[:, 2 * RWKV_WIDTH:3 * RWKV_WIDTH]
    x_lora = zs[:, LORA_OFF:LORA_OFF + LANES]
    xg = zs[:, GATE_LORA_OFF:GATE_LORA_OFF + GATE_LORA]

    h = _rmsnorm(x_ref[...], g_ref[...]).astype(BF16)
    zbuf[0:SUBLANES, :] = z[rows - SUBLANES:, :]

    def project(first, last):
        y = jnp.dot(h, w_ref[:, first:last], preferred_element_type=F32)
        if last <= RWKV_COLS:
            zbuf[SUBLANES:SUBLANES + rows, first:last] = y
        elif last <= GATE_OFF:
            zp_ref[:, first - RWKV_COLS:last - RWKV_COLS] = y
        else:
            zg_ref[:, first - GATE_OFF:last - GATE_OFF] = y

    pieces = iter([(c, min(c + PROJ_PIECE, RWKV_COLS)) for c in range(0, RWKV_COLS, PROJ_PIECE)]
                  + [(c, c + PROJ_PIECE) for c in range(RWKV_COLS, D_IN, PROJ_PIECE)])

    def project_next(n):
        for _ in range(n):
            project(*next(pieces))

    hsum = hsum_ref[...]
    project_next(1)
    dw = _mm(jnp.tanh(x_lora), wd_ref[...])
    da = _mm(x_lora, wa_ref[...])
    rgate_ref[...] = _mm(jax.nn.sigmoid(xg), wg_ref[...])
    kk = k * kk_ref[...]
    ss = _head_sum(kk * kk, hsum)
    project_next(1)
    logw = LOG_DECAY_SCALE * jax.nn.sigmoid(w0_ref[...] + dw)
    a = jax.nn.sigmoid(a0_ref[...] + da)
    k = k * (1.0 + (a - 1.0) * ka_ref[...])
    n_chunks = rows // CHUNK
    tri2 = tri_ref[...]
    cums = [_chunk_cumsum(tri2, logw[c * CHUNK:(c + 1) * CHUNK, :]) for c in range(n_chunks)]
    bonus_sum = _head_sum(r * k * rk_ref[...], hsum)
    project_next(2)

    kk = kk * jnp.minimum(lax.rsqrt(ss), 1.0 / L2_EPS)
    beta = kk * a
    cum = jnp.concatenate(cums, axis=0)
    p_rows = [jnp.exp(x[CHUNK - 1:CHUNK, :]) for x in cums]
    p_full = jnp.concatenate([jnp.broadcast_to(x, (CHUNK, RWKV_WIDTH)) for x in p_rows], axis=0)
    pc_ref[...] = jnp.concatenate(
        [jnp.broadcast_to(x, (SUBLANES, RWKV_WIDTH)) for x in p_rows], axis=0)
    e_neg = jnp.exp(-cum)
    e_tail = p_full * e_neg
    rt_ref[...] = r * jnp.exp(cum)
    at_ref[...] = (-kk * jnp.exp(cum - logw)).astype(BF16)
    bt_ref[...] = (beta * e_neg).astype(BF16)
    kt_ref[...] = (k * e_neg).astype(BF16)
    btail_ref[...] = (beta * e_tail).astype(BF16)
    ktail_ref[...] = (k * e_tail).astype(BF16)
    v_ref[...] = v.astype(BF16)
    bonus_ref[...] = bonus_sum * v
    project_next(5)
    assert next(pieces, None) is None


def _front(x2d, batch, seq, params):
    assert (D_IN - RWKV_COLS) % PROJ_PIECE == 0
    tiles_per_seq = seq // FRONT_ROWS
    n_tiles = batch * tiles_per_seq
    m = batch * seq
    const = lambda arr: pl.BlockSpec(arr.shape, lambda s: (0,) * arr.ndim,
                                     pipeline_mode=pl.Buffered(1))
    cur = lambda s: (jnp.minimum(s, n_tiles - 1), 0)
    lag = lambda s: (jnp.maximum(s - 1, 0), 0)
    lagged = lambda width: pl.BlockSpec((FRONT_ROWS, width), lag)
    pc_rows = FRONT_ROWS // CHUNK * SUBLANES
    f32 = lambda r, w: jax.ShapeDtypeStruct((r, w), F32)
    bf16 = lambda r, w: jax.ShapeDtypeStruct((r, w), BF16)
    w = RWKV_WIDTH
    return pl.pallas_call(
        functools.partial(_front_kernel, rows=FRONT_ROWS, tiles_per_seq=tiles_per_seq),
        grid=(n_tiles + 1,),
        in_specs=[pl.BlockSpec((FRONT_ROWS, D_MODEL), cur)] + [const(p) for p in params],
        out_specs=[lagged(w)] * 7 + [pl.BlockSpec((pc_rows, w), lag)]
                  + [lagged(w), lagged(w),
                     pl.BlockSpec((FRONT_ROWS, POOL_WIDTH), cur),
                     pl.BlockSpec((FRONT_ROWS, GATE_COLS), cur)],
        out_shape=[f32(m, w)] + [bf16(m, w)] * 6 + [f32(m // CHUNK * SUBLANES, w)]
                  + [f32(m, w), f32(m, w), f32(m, POOL_WIDTH), f32(m, GATE_COLS)],
        scratch_shapes=[pltpu.VMEM((FRONT_ROWS + SUBLANES, RWKV_COLS), F32)],
        compiler_params=pltpu.CompilerParams(dimension_semantics=("arbitrary",),
                                             vmem_limit_bytes=VMEM_LIMIT_BYTES),
        name="front",
    )(x2d, *params)


def _recur_kernel(rt_ref, at_ref, bt_ref, kt_ref, v_ref, btail_ref, ktail_ref, pc_ref, bonus_ref,
                  rgate_ref, zp_ref, lnw_ref, lnb_ref, hsum_ref, poolw_ref, pscale_ref, *rest):
    n_cast = N_TAIL_CASTS
    cast_in, (out_ref, yb_ref) = rest[:n_cast], rest[n_cast:n_cast + 2]
    cast_out, (g_state, pbuf, plev) = rest[n_cast + 2:2 * n_cast + 2], rest[2 * n_cast + 2:]
    t = pl.program_id(1)

    for src, dst in zip(cast_in, cast_out):
        dst[...] = src[...].astype(BF16)

    @pl.when(t == 0)
    def _():
        g_state[...] = jnp.zeros(g_state.shape, F32)
        pbuf[0:MAX_WINDOW, :] = jnp.zeros((MAX_WINDOW, POOL_WIDTH), F32)

    pbuf[MAX_WINDOW:MAX_WINDOW + RECUR_ROWS, :] = zp_ref[...]
    pooled = []

    row = lax.broadcasted_iota(jnp.int32, (CHUNK, LANES), 0)
    col = lax.broadcasted_iota(jnp.int32, (CHUNK, LANES), 1) & (HEAD_DIM - 1)
    strict_lower = col < row
    lower = col <= row
    eye = (col == row).astype(F32)
    lane = lax.broadcasted_iota(jnp.int32, (1, LANES), 1)
    head_masks = [(_head_of(lane) == h).astype(BF16) for h in range(HEADS_PER_BLOCK)]
    head_masks2 = [jnp.concatenate([m, m], axis=1) for m in head_masks]
    brow = lax.broadcasted_iota(jnp.int32, (LANES, LANES), 0)
    bcol = lax.broadcasted_iota(jnp.int32, (LANES, LANES), 1)
    same_head = _head_of(brow) == _head_of(bcol)
    diag = brow == bcol

    def block_diag(x):
        xb = x.astype(BF16)
        masks = head_masks if x.shape[1] == LANES else head_masks2
        return jnp.concatenate([xb * m for m in masks], axis=0)

    stack = lambda x, y: jnp.concatenate([x.astype(BF16), y.astype(BF16)], axis=0)
    no_w2 = jnp.zeros((CHUNK, LANES), BF16)

    def block_levels(chunks, out, with_pooling):
        blocks = [(slice(c * CHUNK, (c + 1) * CHUNK), slice(p * LANES, (p + 1) * LANES))
                  for c in chunks for p in range(N_PAIRS)]
        a_ps = [at_ref[blk] for blk in blocks]
        r_ps = [rt_ref[blk] for blk in blocks]
        v_ps = [v_ref[blk] for blk in blocks]
        v_bd = [block_diag(x) for x in v_ps]
        tri4 = [_mm_nt(stack(x, y),
                       jnp.concatenate([block_diag(bt_ref[blk]), block_diag(kt_ref[blk])], axis=0))
                for x, y, blk in zip(a_ps, r_ps, blocks)]
        a_ab = [jnp.where(strict_lower, x[:CHUNK, :LANES], 0.0) for x in tri4]
        a_ak = [jnp.where(strict_lower, x[:CHUNK, LANES:], 0.0) for x in tri4]
        a_rb = [jnp.where(lower, x[CHUNK:, :LANES], 0.0) for x in tri4]
        a_rk = [jnp.where(lower, x[CHUNK:, LANES:], 0.0) for x in tri4]
        yield
        xk = [_mm(stack(x, y), z) for x, y, z in zip(a_ak, a_rk, v_bd)]
        x1 = [x[:CHUNK] for x in xk]
        yk = [x[CHUNK:] for x in xk]
        t_inv = [eye + x for x in a_ab]
        pw = [_mm(x, block_diag(x)) for x in a_ab]
        yield
        for level in range(N_DOUBLINGS - 1):
            both = [_mm(stack(x, y), block_diag(x)) for x, y in zip(pw, t_inv)]
            pw = [x[:CHUNK] for x in both]
            t_inv = [x + y[CHUNK:] for x, y in zip(t_inv, both)]
            if with_pooling and level < len(POOL_WINDOWS):
                pooled.append(_mm(_pool_group(level, zp_ref, pbuf, plev, t, RECUR_ROWS),
                                  poolw_ref[level]))
            yield
        t_inv = [x + _mm(x, block_diag(y)) for x, y in zip(t_inv, pw)]
        yield
        w12 = [_mm(x, block_diag(jnp.concatenate([y.astype(BF16), z], axis=1)))
               for x, y, z in zip(t_inv, x1, a_ps)]
        yield
        p_last = [pc_ref[c * SUBLANES:c * SUBLANES + 1, p * LANES:(p + 1) * LANES]
                  for c in chunks for p in range(N_PAIRS)]
        mn = [_mm_tn(stack(btail_ref[blk], ktail_ref[blk]),
                     jnp.concatenate([jnp.concatenate([w12[q][:, LANES:], w12[q][:, :LANES]], axis=1)
                                      .astype(BF16), jnp.concatenate([no_w2, v_ps[q]], axis=1)], axis=0))
              for q, blk in enumerate(blocks)]
        out["m_t"] = [jnp.where(same_head, x[:, :LANES], 0.0) + jnp.where(diag, p_last[q], 0.0)
                      for q, x in enumerate(mn)]
        out["n_t"] = [jnp.where(same_head, x[:, LANES:], 0.0) for x in mn]
        out.update(a_rb=a_rb, w12=w12, yk=yk, r_ps=r_ps)
        yield

    assert N_DOUBLINGS - 1 >= len(POOL_WINDOWS)
    half = CHUNKS_PER_STEP // 2
    groups = [range(0, half), range(half, CHUNKS_PER_STEP)]
    res = [{}, {}]
    states = [[g_state[p] for p in range(N_PAIRS)]]
    yr = []

    def chain_step(c):
        g = res[c // half]
        qs = range((c % half) * N_PAIRS, (c % half + 1) * N_PAIRS)
        states.append([_mm(g["m_t"][q], states[c][p]) + g["n_t"][q] for p, q in enumerate(qs)])
        yr.extend(_mm(g["a_rb"][q], block_diag(g["w12"][q])) for q in qs)

    for _ in block_levels(groups[0], res[0], True):
        pass
    pbuf[0:MAX_WINDOW, :] = zp_ref[RECUR_ROWS - MAX_WINDOW:, :]
    pending = list(groups[0])
    for _ in block_levels(groups[1], res[1], False):
        if pending:
            chain_step(pending.pop(0))
    for c in pending + list(groups[1]):
        chain_step(c)
    for p in range(N_PAIRS):
        g_state[p] = states[CHUNKS_PER_STEP][p]
    yk = res[0]["yk"] + res[1]["yk"]
    r_ps = res[0]["r_ps"] + res[1]["r_ps"]
    ys = [yk[q] + yr[q][:, :LANES] + _mm(r_ps[q] + yr[q][:, LANES:], states[q // N_PAIRS][q % N_PAIRS])
          for q in range(CHUNKS_PER_STEP * N_PAIRS)]
    y = jnp.concatenate([jnp.concatenate(ys[c * N_PAIRS:(c + 1) * N_PAIRS], axis=1)
                         for c in range(CHUNKS_PER_STEP)], axis=0)

    yb_ref[...] = jnp.concatenate(pooled, axis=1) * pscale_ref[...]

    hsum = hsum_ref[...]
    mean = _head_sum(y, hsum) * (1.0 / HEAD_DIM)
    d = y - mean
    var = _head_sum(d * d, hsum) * (1.0 / HEAD_DIM)
    yn = d * lax.rsqrt(var + GN_EPS) * lnw_ref[...] + lnb_ref[...]
    out_ref[...] = (yn + bonus_ref[...]) * rgate_ref[...]


def _recur(rt, at, bt, kt, v, btail, ktail, pc, bonus, rgate, zp, batch, seq, params, tail_weights):
    assert RWKV_WIDTH == POOL_WIDTH
    assert len(tail_weights) == N_TAIL_CASTS
    n_t = seq // RECUR_ROWS
    n_steps = batch * n_t
    step = lambda b, t: (b * n_t + t, 0)
    const = lambda arr: pl.BlockSpec(arr.shape, lambda b, t: (0,) * arr.ndim,
                                     pipeline_mode=pl.Buffered(1))
    rows = pl.BlockSpec((RECUR_ROWS, RWKV_WIDTH), step)
    pc_rows = pl.BlockSpec((CHUNKS_PER_STEP * SUBLANES, RWKV_WIDTH), step)
    out = jax.ShapeDtypeStruct((batch * seq, RWKV_WIDTH), F32)
    for w in tail_weights:
        assert w.shape[0] % (n_steps * 2 * SUBLANES) == 0, w.shape
    w_slices = [pl.BlockSpec((w.shape[0] // n_steps, w.shape[1]), step) for w in tail_weights]
    results = pl.pallas_call(
        _recur_kernel,
        grid=(batch, n_t),
        in_specs=[rows] * 7 + [pc_rows, rows, rows, rows] + [const(p) for p in params] + w_slices,
        out_specs=[rows, rows] + w_slices,
        out_shape=[out, out] + [jax.ShapeDtypeStruct(w.shape, BF16) for w in tail_weights],
        scratch_shapes=[pltpu.VMEM((N_PAIRS, LANES, LANES), F32),
                        pltpu.VMEM((RECUR_ROWS + MAX_WINDOW, POOL_WIDTH), F32),
                        pltpu.VMEM((2, RECUR_ROWS + MAX_WINDOW, POOL_WIDTH), F32)],
        compiler_params=pltpu.CompilerParams(dimension_semantics=("arbitrary", "arbitrary"),
                                             vmem_limit_bytes=VMEM_LIMIT_BYTES),
        name="recur",
    )(rt, at, bt, kt, v, btail, ktail, pc, bonus, rgate, zp, *params, *tail_weights)
    return results[0], results[1], results[2:]


def _tail_kernel(x_ref, ya_ref, yb_ref, zg_ref, p_ref, bg_ref, woa_ref, wob_ref,
                 wo_ref, gmlp_ref, w1_ref, w2_ref, gple_ref, wpg_ref, wpe_ref, gfin_ref, out_ref):
    y_a = _mm(ya_ref[...], woa_ref[...])
    y_b = _mm(yb_ref[...], wob_ref[...])
    gate = jax.nn.sigmoid(zg_ref[...] + bg_ref[...])
    merged = gate[:, :D_MODEL] * y_a + gate[:, D_MODEL:] * y_b
    x = x_ref[...] + _mm(merged, wo_ref[...])

    ple_lo = _mm(p_ref[...], wpe_ref[:, :D_MODEL // 2])
    f = jnp.maximum(_mm(_rmsnorm(x, gmlp_ref[...]), w1_ref[...]), 0.0)
    x = x + _mm(f * f, w2_ref[...])
    ple = jnp.concatenate([ple_lo, _mm(p_ref[...], wpe_ref[:, D_MODEL // 2:])], axis=1)

    n = x.shape[0]
    halves = (slice(0, n // 2), slice(n // 2, n))
    gate_pre = [_mm(_rmsnorm(x[r], gple_ref[...]), wpg_ref[...]) for r in halves]
    for r, pre in zip(halves, gate_pre):
        out_ref[r, :] = _rmsnorm(x[r] + jax.nn.sigmoid(pre) * ple[r], gfin_ref[...])


def _tail(x2d, ya, yb, zg, p2d, params):
    m = x2d.shape[0]
    rows = lambda width: pl.BlockSpec((TAIL_ROWS, width), lambda i: (i, 0))
    const = lambda arr: pl.BlockSpec(arr.shape, lambda i: (0,) * arr.ndim,
                                     pipeline_mode=pl.Buffered(1))
    return pl.pallas_call(
        _tail_kernel,
        grid=(m // TAIL_ROWS,),
        in_specs=[rows(D_MODEL), rows(RWKV_WIDTH), rows(POOL_WIDTH), rows(GATE_COLS), rows(PLE_DIM)]
                 + [const(p) for p in params],
        out_specs=rows(D_MODEL),
        out_shape=jax.ShapeDtypeStruct((m, D_MODEL), F32),
        compiler_params=pltpu.CompilerParams(dimension_semantics=("arbitrary",),
                                             vmem_limit_bytes=VMEM_LIMIT_BYTES),
        name="tail",
    )(x2d, ya, yb, zg, p2d, *params)


def _pad_rows(w, before, total):
    return jnp.pad(w, ((before, total - before - w.shape[0]), (0, 0)))


def kernel(x, p, g_mix, w_in, mu_shift, w0, w_decay_up, a0, w_aaa_up, w_gate_up, k_k, k_a, r_k, ln_x_w, ln_x_b, pool_w, pool_scale, b_gates, w_out_a, w_out_b, w_o, g_mlp, w_ff1, w_ff2, g_ple, w_ple_gate, w_ple_proj, g_final):
    batch, seq, d = x.shape
    depth = w_in.shape[0]
    assert d == D_MODEL and seq % FRONT_ROWS == 0 and seq % RECUR_ROWS == 0
    assert (batch * seq) % TAIL_ROWS == 0
    assert depth == 1, "the fused tail kernel supports a single layer"
    i = 0
    row = lambda a: a.reshape(1, -1)
    bf = lambda a: a.astype(BF16)

    head_id = jnp.arange(HSUM_WIDTH) // HEAD_DIM
    hsum = (head_id[:, None] == head_id[None, :]).astype(BF16)
    tok = jnp.arange(CHUNK)
    tri = (tok[None, :] <= tok[:, None]).astype(BF16)
    tri = jnp.concatenate([tri, tri], axis=1)

    x2d = x.reshape(batch * seq, d)
    front_params = (row(g_mix[i]), bf(w_in[i]), row(mu_shift[i]), row(w0[i]),
                    bf(_pad_rows(w_decay_up[i], 0, LANES)), row(a0[i]),
                    bf(_pad_rows(w_aaa_up[i], DECAY_LORA, LANES)), bf(w_gate_up[i]),
                    row(k_k[i]), row(k_a[i]), row(r_k[i]), hsum, tri)
    (rt, at, bt, kt, v, btail, ktail, pc, bonus, rgate, zp, zg) = _front(x2d, batch, seq,
                                                                         front_params)
    recur_params = (row(ln_x_w[i]), row(ln_x_b[i]), hsum, bf(pool_w[i]), row(pool_scale[i]))
    ya, yb, (wo_b, wff1_b, wff2_b, wpg_b) = _recur(
        rt, at, bt, kt, v, btail, ktail, pc, bonus, rgate, zp, batch, seq, recur_params,
        (w_o[i], w_ff1[i], w_ff2[i], w_ple_gate[i]))
    tail_params = (row(b_gates[i]), bf(w_out_a[i]), bf(w_out_b[i]), wo_b, row(g_mlp[i]),
                   wff1_b, wff2_b, row(g_ple[i]), wpg_b, bf(w_ple_proj[i]), row(g_final))
    out = _tail(x2d, ya, yb, zg, p[i].reshape(batch * seq, PLE_DIM), tail_params)
    return out.reshape(batch, seq, d)
```

```python
import functools
import math

import jax
import jax.numpy as jnp
from jax import lax
from jax.experimental import pallas as pl
from jax.experimental.pallas import tpu as pltpu

F32 = jnp.float32
BF16 = jnp.bfloat16

D_MODEL = 1024
PLE_DIM = 256
HEAD_DIM = 64
RWKV_WIDTH = 512
HEAD_SHIFT = int(math.log2(HEAD_DIM))
DECAY_LORA = 64
AAA_LORA = 64
GATE_LORA = 128
POOL_WINDOWS = (2, 4, 8, 16)
POOL_WIDTH = 512
POOL_GROUP_DIM = POOL_WIDTH // len(POOL_WINDOWS)
MAX_WINDOW = max(POOL_WINDOWS)
RMS_EPS = 1e-6
GN_EPS = 64e-5
L2_EPS = 1e-12
RWKV_COLS = 3 * RWKV_WIDTH + DECAY_LORA + AAA_LORA + GATE_LORA
GATE_COLS = 2 * D_MODEL
D_IN = RWKV_COLS + POOL_WIDTH + GATE_COLS
LORA_OFF = 3 * RWKV_WIDTH
GATE_LORA_OFF = LORA_OFF + DECAY_LORA + AAA_LORA
GATE_OFF = RWKV_COLS + POOL_WIDTH

LANES = 128
SUBLANES = 8
HEADS_PER_BLOCK = LANES // HEAD_DIM
N_PAIRS = RWKV_WIDTH // LANES
HSUM_WIDTH = 256
CHUNK = 64
CHUNKS_PER_STEP = 8
RECUR_ROWS = CHUNK * CHUNKS_PER_STEP
N_DOUBLINGS = int(math.log2(CHUNK)) - 1
LOG_DECAY_SCALE = -math.exp(-0.5)

FRONT_ROWS = 512
PROJ_PIECE = 512
TAIL_ROWS = 512
N_TAIL_CASTS = 4
VMEM_LIMIT_BYTES = 56 * 1024 * 1024


def _head_of(channel):
    return lax.shift_right_logical(channel, HEAD_SHIFT)


def _mm(a, b):
    return jnp.dot(a.astype(BF16), b.astype(BF16), preferred_element_type=F32)


def _mm_nt(a, b):
    return lax.dot_general(a.astype(BF16), b.astype(BF16), (((1,), (1,)), ((), ())),
                           preferred_element_type=F32)


def _mm_tn(a, b):
    return lax.dot_general(a.astype(BF16), b.astype(BF16), (((0,), (0,)), ((), ())),
                           preferred_element_type=F32)


def _chunk_cumsum(tri2, x):
    hi = x.astype(BF16)
    lo = (x - hi.astype(F32)).astype(BF16)
    return jnp.dot(tri2, jnp.concatenate([hi, lo], axis=0), preferred_element_type=F32)


def _head_sum(y, hsum):
    return jnp.concatenate(
        [_mm(y[:, i:i + HSUM_WIDTH], hsum) for i in range(0, RWKV_WIDTH, HSUM_WIDTH)], axis=1)


def _rmsnorm(x, g):
    return x * lax.rsqrt(jnp.mean(x * x, axis=-1, keepdims=True) + RMS_EPS) * g


def _pool_group(gi, zp_ref, pbuf, plev, t, rows):
    win = POOL_WINDOWS[gi]
    n_levels = win.bit_length() - 1
    assert win == 1 << n_levels and win <= MAX_WINDOW
    cols = slice(gi * POOL_GROUP_DIM, (gi + 1) * POOL_GROUP_DIM)
    tok = zp_ref[:, cols]
    end = MAX_WINDOW + rows
    prev = pbuf
    for level in range(1, n_levels + 1):
        shift = 1 << (level - 1)
        first = MAX_WINDOW - (win - (1 << level))
        cur = plev.at[level % 2]
        cur[first:end, cols] = prev[first:end, cols] + prev[first - shift:end - shift, cols]
        prev = cur
    acc = prev[MAX_WINDOW:end, cols]
    pos = t * rows + lax.broadcasted_iota(jnp.int32, (rows, 1), 0)
    count = jnp.minimum(pos + 1, win).astype(F32)
    return acc / count - tok


def _front_kernel(x_ref, g_ref, w_ref, mu_ref, w0_ref, wd_ref, a0_ref, wa_ref, wg_ref,
                  kk_ref, ka_ref, rk_ref, hsum_ref, tri_ref,
                  rt_ref, at_ref, bt_ref, kt_ref, v_ref, btail_ref, ktail_ref, pc_ref,
                  bonus_ref, rgate_ref, zp_ref, zg_ref, zbuf, *, rows, tiles_per_seq):
    s = pl.program_id(0)
    t_lag = jnp.maximum(s - 1, 0) % tiles_per_seq

    @pl.when(s == 0)
    def _():
        zbuf[...] = jnp.zeros(zbuf.shape, F32)

    @pl.when(t_lag == 0)
    def _():
        zbuf[0:SUBLANES, :] = jnp.zeros((SUBLANES, RWKV_COLS), F32)

    z = zbuf[SUBLANES:SUBLANES + rows, :]
    z_prev = zbuf[SUBLANES - 1:SUBLANES - 1 + rows, :]
    zs = z + (z_prev - z) * mu_ref[...]
    r = zs[:, 0:RWKV_WIDTH]
    k = zs[:, RWKV_WIDTH:2 * RWKV_WIDTH]
    v = zs[:, 2 * RWKV_WIDTH:3 * RWKV_WIDTH]
    x_lora = zs[:, LORA_OFF:LORA_OFF + LANES]
    xg = zs[:, GATE_LORA_OFF:GATE_LORA_OFF + GATE_LORA]

    h = _rmsnorm(x_ref[...], g_ref[...]).astype(BF16)
    zbuf[0:SUBLANES, :] = z[rows - SUBLANES:, :]

    def project(first, last):
        y = jnp.dot(h, w_ref[:, first:last], preferred_element_type=F32)
        if last <= RWKV_COLS:
            zbuf[SUBLANES:SUBLANES + rows, first:last] = y
        elif last <= GATE_OFF:
            zp_ref[:, first - RWKV_COLS:last - RWKV_COLS] = y
        else:
            zg_ref[:, first - GATE_OFF:last - GATE_OFF] = y

    pieces = iter([(c, min(c + PROJ_PIECE, RWKV_COLS)) for c in range(0, RWKV_COLS, PROJ_PIECE)]
                  + [(c, c + PROJ_PIECE) for c in range(RWKV_COLS, D_IN, PROJ_PIECE)])

    def project_next(n):
        for _ in range(n):
            project(*next(pieces))

    hsum = hsum_ref[...]
    project_next(1)
    dw = _mm(jnp.tanh(x_lora), wd_ref[...])
    da = _mm(x_lora, wa_ref[...])
    rgate_ref[...] = _mm(jax.nn.sigmoid(xg), wg_ref[...])
    kk = k * kk_ref[...]
    ss = _head_sum(kk * kk, hsum)
    project_next(1)
    logw = LOG_DECAY_SCALE * jax.nn.sigmoid(w0_ref[...] + dw)
    a = jax.nn.sigmoid(a0_ref[...] + da)
    k = k * (1.0 + (a - 1.0) * ka_ref[...])
    n_chunks = rows // CHUNK
    tri2 = tri_ref[...]
    cums = [_chunk_cumsum(tri2, logw[c * CHUNK:(c + 1) * CHUNK, :]) for c in range(n_chunks)]
    bonus_sum = _head_sum(r * k * rk_ref[...], hsum)
    project_next(2)

    kk = kk * jnp.minimum(lax.rsqrt(ss), 1.0 / L2_EPS)
    beta = kk * a
    cum = jnp.concatenate(cums, axis=0)
    p_rows = [jnp.exp(x[CHUNK - 1:CHUNK, :]) for x in cums]
    p_full = jnp.concatenate([jnp.broadcast_to(x, (CHUNK, RWKV_WIDTH)) for x in p_rows], axis=0)
    pc_ref[...] = jnp.concatenate(
        [jnp.broadcast_to(x, (SUBLANES, RWKV_WIDTH)) for x in p_rows], axis=0)
    e_neg = jnp.exp(-cum)
    e_tail = p_full * e_neg
    rt_ref[...] = r * jnp.exp(cum)
    at_ref[...] = (-kk * jnp.exp(cum - logw)).astype(BF16)
    bt_ref[...] = (beta * e_neg).astype(BF16)
    kt_ref[...] = (k * e_neg).astype(BF16)
    btail_ref[...] = (beta * e_tail).astype(BF16)
    ktail_ref[...] = (k * e_tail).astype(BF16)
    v_ref[...] = v.astype(BF16)
    bonus_ref[...] = bonus_sum * v
    project_next(5)
    assert next(pieces, None) is None


def _front(x2d, batch, seq, params):
    assert (D_IN - RWKV_COLS) % PROJ_PIECE == 0
    tiles_per_seq = seq // FRONT_ROWS
    n_tiles = batch * tiles_per_seq
    m = batch * seq
    const = lambda arr: pl.BlockSpec(arr.shape, lambda s: (0,) * arr.ndim,
                                     pipeline_mode=pl.Buffered(1))
    cur = lambda s: (jnp.minimum(s, n_tiles - 1), 0)
    lag = lambda s: (jnp.maximum(s - 1, 0), 0)
    lagged = lambda width: pl.BlockSpec((FRONT_ROWS, width), lag)
    pc_rows = FRONT_ROWS // CHUNK * SUBLANES
    f32 = lambda r, w: jax.ShapeDtypeStruct((r, w), F32)
    bf16 = lambda r, w: jax.ShapeDtypeStruct((r, w), BF16)
    w = RWKV_WIDTH
    return pl.pallas_call(
        functools.partial(_front_kernel, rows=FRONT_ROWS, tiles_per_seq=tiles_per_seq),
        grid=(n_tiles + 1,),
        in_specs=[pl.BlockSpec((FRONT_ROWS, D_MODEL), cur)] + [const(p) for p in params],
        out_specs=[lagged(w)] * 7 + [pl.BlockSpec((pc_rows, w), lag)]
                  + [lagged(w), lagged(w),
                     pl.BlockSpec((FRONT_ROWS, POOL_WIDTH), cur),
                     pl.BlockSpec((FRONT_ROWS, GATE_COLS), cur)],
        out_shape=[f32(m, w)] + [bf16(m, w)] * 6 + [f32(m // CHUNK * SUBLANES, w)]
                  + [f32(m, w), f32(m, w), f32(m, POOL_WIDTH), f32(m, GATE_COLS)],
        scratch_shapes=[pltpu.VMEM((FRONT_ROWS + SUBLANES, RWKV_COLS), F32)],
        compiler_params=pltpu.CompilerParams(dimension_semantics=("arbitrary",),
                                             vmem_limit_bytes=VMEM_LIMIT_BYTES),
        name="front",
    )(x2d, *params)


def _recur_kernel(rt_ref, at_ref, bt_ref, kt_ref, v_ref, btail_ref, ktail_ref, pc_ref, bonus_ref,
                  rgate_ref, zp_ref, lnw_ref, lnb_ref, hsum_ref, poolw_ref, pscale_ref, *rest):
    n_cast = N_TAIL_CASTS
    cast_in, (out_ref, yb_ref) = rest[:n_cast], rest[n_cast:n_cast + 2]
    cast_out, (g_state, pbuf, plev) = rest[n_cast + 2:2 * n_cast + 2], rest[2 * n_cast + 2:]
    t = pl.program_id(1)

    for src, dst in zip(cast_in, cast_out):
        dst[...] = src[...].astype(BF16)

    @pl.when(t == 0)
    def _():
        g_state[...] = jnp.zeros(g_state.shape, F32)
        pbuf[0:MAX_WINDOW, :] = jnp.zeros((MAX_WINDOW, POOL_WIDTH), F32)

    pbuf[MAX_WINDOW:MAX_WINDOW + RECUR_ROWS, :] = zp_ref[...]
    pooled = []

    row = lax.broadcasted_iota(jnp.int32, (CHUNK, LANES), 0)
    col = lax.broadcasted_iota(jnp.int32, (CHUNK, LANES), 1) & (HEAD_DIM - 1)
    strict_lower = col < row
    lower = col <= row
    eye = (col == row).astype(F32)
    lane = lax.broadcasted_iota(jnp.int32, (1, LANES), 1)
    head_masks = [(_head_of(lane) == h).astype(BF16) for h in range(HEADS_PER_BLOCK)]
    head_masks2 = [jnp.concatenate([m, m], axis=1) for m in head_masks]
    brow = lax.broadcasted_iota(jnp.int32, (LANES, LANES), 0)
    bcol = lax.broadcasted_iota(jnp.int32, (LANES, LANES), 1)
    same_head = _head_of(brow) == _head_of(bcol)
    diag = brow == bcol

    def block_diag(x):
        xb = x.astype(BF16)
        masks = head_masks if x.shape[1] == LANES else head_masks2
        return jnp.concatenate([xb * m for m in masks], axis=0)

    stack = lambda x, y: jnp.concatenate([x.astype(BF16), y.astype(BF16)], axis=0)
    no_w2 = jnp.zeros((CHUNK, LANES), BF16)

    def block_levels(chunks, out, with_pooling):
        blocks = [(slice(c * CHUNK, (c + 1) * CHUNK), slice(p * LANES, (p + 1) * LANES))
                  for c in chunks for p in range(N_PAIRS)]
        a_ps = [at_ref[blk] for blk in blocks]
        r_ps = [rt_ref[blk] for blk in blocks]
        v_ps = [v_ref[blk] for blk in blocks]
        v_bd = [block_diag(x) for x in v_ps]
        tri4 = [_mm_nt(stack(x, y),
                       jnp.concatenate([block_diag(bt_ref[blk]), block_diag(kt_ref[blk])], axis=0))
                for x, y, blk in zip(a_ps, r_ps, blocks)]
        a_ab = [jnp.where(strict_lower, x[:CHUNK, :LANES], 0.0) for x in tri4]
        a_ak = [jnp.where(strict_lower, x[:CHUNK, LANES:], 0.0) for x in tri4]
        a_rb = [jnp.where(lower, x[CHUNK:, :LANES], 0.0) for x in tri4]
        a_rk = [jnp.where(lower, x[CHUNK:, LANES:], 0.0) for x in tri4]
        yield
        xk = [_mm(stack(x, y), z) for x, y, z in zip(a_ak, a_rk, v_bd)]
        x1 = [x[:CHUNK] for x in xk]
        yk = [x[CHUNK:] for x in xk]
        t_inv = [eye + x for x in a_ab]
        pw = [_mm(x, block_diag(x)) for x in a_ab]
        yield
        for level in range(N_DOUBLINGS - 1):
            both = [_mm(stack(x, y), block_diag(x)) for x, y in zip(pw, t_inv)]
            pw = [x[:CHUNK] for x in both]
            t_inv = [x + y[CHUNK:] for x, y in zip(t_inv, both)]
            if with_pooling and level < len(POOL_WINDOWS):
                pooled.append(_mm(_pool_group(level, zp_ref, pbuf, plev, t, RECUR_ROWS),
                                  poolw_ref[level]))
            yield
        t_inv = [x + _mm(x, block_diag(y)) for x, y in zip(t_inv, pw)]
        yield
        w12 = [_mm(x, block_diag(jnp.concatenate([y.astype(BF16), z], axis=1)))
               for x, y, z in zip(t_inv, x1, a_ps)]
        yield
        p_last = [pc_ref[c * SUBLANES:c * SUBLANES + 1, p * LANES:(p + 1) * LANES]
                  for c in chunks for p in range(N_PAIRS)]
        mn = [_mm_tn(stack(btail_ref[blk], ktail_ref[blk]),
                     jnp.concatenate([jnp.concatenate([w12[q][:, LANES:], w12[q][:, :LANES]], axis=1)
                                      .astype(BF16), jnp.concatenate([no_w2, v_ps[q]], axis=1)], axis=0))
              for q, blk in enumerate(blocks)]
        out["m_t"] = [jnp.where(same_head, x[:, :LANES], 0.0) + jnp.where(diag, p_last[q], 0.0)
                      for q, x in enumerate(mn)]
        out["n_t"] = [jnp.where(same_head, x[:, LANES:], 0.0) for x in mn]
        yr = [_mm(x, block_diag(y)) for x, y in zip(a_rb, w12)]
        out["y1"] = [x + y[:, :LANES] for x, y in zip(yk, yr)]
        out["r2"] = [x + y[:, LANES:] for x, y in zip(r_ps, yr)]
        yield

    assert N_DOUBLINGS - 1 >= len(POOL_WINDOWS)
    half = CHUNKS_PER_STEP // 2
    groups = [range(0, half), range(half, CHUNKS_PER_STEP)]
    res = [{}, {}]
    states = [[g_state[p] for p in range(N_PAIRS)]]
    ys = []

    def chain_step(c):
        g = res[c // half]
        qs = range((c % half) * N_PAIRS, (c % half + 1) * N_PAIRS)
        both = [_mm(stack(g["m_t"][q], g["r2"][q]), states[c][p]) for p, q in enumerate(qs)]
        states.append([x[:LANES] + g["n_t"][q] for x, q in zip(both, qs)])
        ys.extend(g["y1"][q] + x[LANES:] for x, q in zip(both, qs))

    for _ in block_levels(groups[0], res[0], True):
        pass
    pbuf[0:MAX_WINDOW, :] = zp_ref[RECUR_ROWS - MAX_WINDOW:, :]
    pending = list(groups[0])
    for _ in block_levels(groups[1], res[1], False):
        if pending:
            chain_step(pending.pop(0))
    for c in pending + list(groups[1]):
        chain_step(c)
    for p in range(N_PAIRS):
        g_state[p] = states[CHUNKS_PER_STEP][p]
    y = jnp.concatenate([jnp.concatenate(ys[c * N_PAIRS:(c + 1) * N_PAIRS], axis=1)
                         for c in range(CHUNKS_PER_STEP)], axis=0)

    yb_ref[...] = jnp.concatenate(pooled, axis=1) * pscale_ref[...]

    hmean = hsum_ref[...]
    d = y - _head_sum(y, hmean)
    var = _head_sum(d * d, hmean)
    yn = d * lax.rsqrt(var + GN_EPS) * lnw_ref[...] + lnb_ref[...]
    out_ref[...] = (yn + bonus_ref[...]) * rgate_ref[...]


def _recur(rt, at, bt, kt, v, btail, ktail, pc, bonus, rgate, zp, batch, seq, params, tail_weights):
    assert RWKV_WIDTH == POOL_WIDTH
    assert len(tail_weights) == N_TAIL_CASTS
    n_t = seq // RECUR_ROWS
    n_steps = batch * n_t
    step = lambda b, t: (b * n_t + t, 0)
    const = lambda arr: pl.BlockSpec(arr.shape, lambda b, t: (0,) * arr.ndim,
                                     pipeline_mode=pl.Buffered(1))
    rows = pl.BlockSpec((RECUR_ROWS, RWKV_WIDTH), step)
    pc_rows = pl.BlockSpec((CHUNKS_PER_STEP * SUBLANES, RWKV_WIDTH), step)
    out = jax.ShapeDtypeStruct((batch * seq, RWKV_WIDTH), F32)
    for w in tail_weights:
        assert w.shape[0] % (n_steps * 2 * SUBLANES) == 0, w.shape
    w_slices = [pl.BlockSpec((w.shape[0] // n_steps, w.shape[1]), step) for w in tail_weights]
    results = pl.pallas_call(
        _recur_kernel,
        grid=(batch, n_t),
        in_specs=[rows] * 7 + [pc_rows, rows, rows, rows] + [const(p) for p in params] + w_slices,
        out_specs=[rows, rows] + w_slices,
        out_shape=[out, out] + [jax.ShapeDtypeStruct(w.shape, BF16) for w in tail_weights],
        scratch_shapes=[pltpu.VMEM((N_PAIRS, LANES, LANES), F32),
                        pltpu.VMEM((RECUR_ROWS + MAX_WINDOW, POOL_WIDTH), F32),
                        pltpu.VMEM((2, RECUR_ROWS + MAX_WINDOW, POOL_WIDTH), F32)],
        compiler_params=pltpu.CompilerParams(dimension_semantics=("arbitrary", "arbitrary"),
                                             vmem_limit_bytes=VMEM_LIMIT_BYTES),
        name="recur",
    )(rt, at, bt, kt, v, btail, ktail, pc, bonus, rgate, zp, *params, *tail_weights)
    return results[0], results[1], results[2:]


def _tail_kernel(x_ref, ya_ref, yb_ref, zg_ref, p_ref, bg_ref, woa_ref, wob_ref,
                 wo_ref, gmlp_ref, w1_ref, w2_ref, gple_ref, wpg_ref, wpe_ref, gfin_ref, out_ref):
    y_a = _mm(ya_ref[...], woa_ref[...])
    y_b = _mm(yb_ref[...], wob_ref[...])
    gate = jax.nn.sigmoid(zg_ref[...] + bg_ref[...])
    merged = gate[:, :D_MODEL] * y_a + gate[:, D_MODEL:] * y_b
    x = x_ref[...] + _mm(merged, wo_ref[...])

    ple_lo = _mm(p_ref[...], wpe_ref[:, :D_MODEL // 2])
    f = jnp.maximum(_mm(_rmsnorm(x, gmlp_ref[...]), w1_ref[...]), 0.0)
    x = x + _mm(f * f, w2_ref[...])
    ple = jnp.concatenate([ple_lo, _mm(p_ref[...], wpe_ref[:, D_MODEL // 2:])], axis=1)

    n = x.shape[0]
    halves = (slice(0, n // 2), slice(n // 2, n))
    gate_pre = [_mm(_rmsnorm(x[r], gple_ref[...]), wpg_ref[...]) for r in halves]
    for r, pre in zip(halves, gate_pre):
        out_ref[r, :] = _rmsnorm(x[r] + jax.nn.sigmoid(pre) * ple[r], gfin_ref[...])


def _tail(x2d, ya, yb, zg, p2d, params):
    m = x2d.shape[0]
    rows = lambda width: pl.BlockSpec((TAIL_ROWS, width), lambda i: (i, 0))
    const = lambda arr: pl.BlockSpec(arr.shape, lambda i: (0,) * arr.ndim,
                                     pipeline_mode=pl.Buffered(1))
    return pl.pallas_call(
        _tail_kernel,
        grid=(m // TAIL_ROWS,),
        in_specs=[rows(D_MODEL), rows(RWKV_WIDTH), rows(POOL_WIDTH), rows(GATE_COLS), rows(PLE_DIM)]
                 + [const(p) for p in params],
        out_specs=rows(D_MODEL),
        out_shape=jax.ShapeDtypeStruct((m, D_MODEL), F32),
        compiler_params=pltpu.CompilerParams(dimension_semantics=("arbitrary",),
                                             vmem_limit_bytes=VMEM_LIMIT_BYTES),
        name="tail",
    )(x2d, ya, yb, zg, p2d, *params)


def _pad_rows(w, before, total):
    return jnp.pad(w, ((before, total - before - w.shape[0]), (0, 0)))


def kernel(x, p, g_mix, w_in, mu_shift, w0, w_decay_up, a0, w_aaa_up, w_gate_up, k_k, k_a, r_k, ln_x_w, ln_x_b, pool_w, pool_scale, b_gates, w_out_a, w_out_b, w_o, g_mlp, w_ff1, w_ff2, g_ple, w_ple_gate, w_ple_proj, g_final):
    batch, seq, d = x.shape
    depth = w_in.shape[0]
    assert d == D_MODEL and seq % FRONT_ROWS == 0 and seq % RECUR_ROWS == 0
    assert (batch * seq) % TAIL_ROWS == 0
    assert depth == 1, "the fused tail kernel supports a single layer"
    i = 0
    row = lambda a: a.reshape(1, -1)
    bf = lambda a: a.astype(BF16)

    head_id = jnp.arange(HSUM_WIDTH) // HEAD_DIM
    hsum = (head_id[:, None] == head_id[None, :]).astype(BF16)
    tok = jnp.arange(CHUNK)
    tri = (tok[None, :] <= tok[:, None]).astype(BF16)
    tri = jnp.concatenate([tri, tri], axis=1)

    x2d = x.reshape(batch * seq, d)
    front_params = (row(g_mix[i]), bf(w_in[i]), row(mu_shift[i]), row(w0[i]),
                    bf(_pad_rows(w_decay_up[i], 0, LANES)), row(a0[i]),
                    bf(_pad_rows(w_aaa_up[i], DECAY_LORA, LANES)), bf(w_gate_up[i]),
                    row(k_k[i]), row(k_a[i]), row(r_k[i]), hsum, tri)
    (rt, at, bt, kt, v, btail, ktail, pc, bonus, rgate, zp, zg) = _front(x2d, batch, seq,
                                                                         front_params)
    recur_params = (row(ln_x_w[i]), row(ln_x_b[i]), hsum * (1.0 / HEAD_DIM), bf(pool_w[i]),
                    row(pool_scale[i]))
    ya, yb, (wo_b, wff1_b, wff2_b, wpg_b) = _recur(
        rt, at, bt, kt, v, btail, ktail, pc, bonus, rgate, zp, batch, seq, recur_params,
        (w_o[i], w_ff1[i], w_ff2[i], w_ple_gate[i]))
    tail_params = (row(b_gates[i]), bf(w_out_a[i]), bf(w_out_b[i]), wo_b, row(g_mlp[i]),
                   wff1_b, wff2_b, row(g_ple[i]), wpg_b, bf(w_ple_proj[i]), row(g_final))
    out = _tail(x2d, ya, yb, zg, p[i].reshape(batch * seq, PLE_DIM), tail_params)
    return out.reshape(batch, seq, d)
```
